```python
import jax, jax.numpy as jnp
from jax import lax
import numpy as np

D_MODEL = 1024
BATCH = 16
SEQ = 2048
DEPTH = 2

GRID_W = 64
CTX_LEN = 256
D_MIX = D_MODEL
N_GROUPS_MIX = 4
GROUP_W = D_MIX // N_GROUPS_MIX
D_FF = 4 * D_MODEL
NORM_EPS = 1e-6
GN_EPS = 1e-5
CONV_K = 31
CONV_NORM_GROUPS = 4
RW_HEAD = 64
RW_HEADS = GROUP_W // RW_HEAD
RW_DECAY_LORA = 64
RW_A_LORA = 64
RW_GATE_LORA = 128
RW_LNX_EPS = 64e-5
POOL_WINDOWS = (2, 4, 8, 16)
POOL_CH = GROUP_W // len(POOL_WINDOWS)
MLA_HEADS = 4
MLA_NOPE = 64
MLA_ROPE = 32
MLA_V = 64
MLA_Q_RANK = 256
MLA_KV_RANK = 128
ROPE_BASE = 10000.0
Q_BLOCK = 128
N_CONV_IN = 2 * GROUP_W
N_POOL_IN = GROUP_W
N_Q_IN = MLA_Q_RANK
N_RW_IN = 3 * GROUP_W + RW_GATE_LORA + 2 * RW_DECAY_LORA + 2 * RW_A_LORA
N_KV_IN = MLA_KV_RANK + MLA_ROPE
OFF_POOL = N_CONV_IN
OFF_Q = OFF_POOL + N_POOL_IN
OFF_RW = OFF_Q + N_Q_IN
OFF_KV = OFF_RW + N_RW_IN
P_IN = OFF_KV + N_KV_IN
RW_R = 0
RW_K = GROUP_W
RW_V = 2 * GROUP_W
RW_G = 3 * GROUP_W
RW_W = RW_G + RW_GATE_LORA
RW_A = RW_W + 2 * RW_DECAY_LORA

kernel_name = "hybrid_parallel_mixer_prefix_dit"


def rms_norm(x, g, eps=NORM_EPS):
    xf = x.astype(jnp.float32)
    y = xf * lax.rsqrt(jnp.mean(xf * xf, axis=-1, keepdims=True) + eps)
    return (y * g.astype(jnp.float32)).astype(x.dtype)


def group_norm(x, n_groups, g, b, eps):
    shp = x.shape
    xf = x.astype(jnp.float32).reshape(*shp[:-1], n_groups, shp[-1] // n_groups)
    mu = jnp.mean(xf, axis=-1, keepdims=True)
    var = jnp.mean(jnp.square(xf - mu), axis=-1, keepdims=True)
    y = ((xf - mu) * lax.rsqrt(var + eps)).reshape(shp)
    return (y * g.astype(jnp.float32) + b.astype(jnp.float32)).astype(x.dtype)


def conv_module(p, dw, db, gn_g, gn_b, pw):
    u = p[..., :GROUP_W] * jax.nn.sigmoid(p[..., GROUP_W:])
    y = lax.conv_general_dilated(u, dw[:, None, :].astype(u.dtype), window_strides=(1,),
                                 padding=[(CONV_K // 2, CONV_K // 2)],
                                 dimension_numbers=("NWC", "WIO", "NWC"),
                                 feature_group_count=GROUP_W)
    y = jax.nn.silu(group_norm(y + db.astype(y.dtype), CONV_NORM_GROUPS, gn_g, gn_b, GN_EPS))
    return y @ pw.astype(y.dtype)


def multiscale_pool(p, pool_w, pool_scale):
    b, n, _ = p.shape
    u = p.astype(jnp.float32).reshape(b, n, len(POOL_WINDOWS), POOL_CH)
    cs = jnp.concatenate([jnp.zeros_like(u[:, :1]), jnp.cumsum(u, axis=1)], axis=1)
    t = jnp.arange(n)
    means = []
    for gi, w in enumerate(POOL_WINDOWS):
        lo = jnp.clip(t - w // 2, 0, n)
        hi = jnp.clip(t + w - w // 2, 0, n)
        cnt = (hi - lo).astype(jnp.float32)
        means.append((cs[:, hi, gi] - cs[:, lo, gi]) / cnt[None, :, None])
    d = jnp.stack(means, axis=2) - u
    y = jnp.einsum("bngc,gcd->bngd", d, pool_w.astype(jnp.float32)).reshape(b, n, GROUP_W)
    return (y * pool_scale.astype(jnp.float32)).astype(p.dtype)


def shift_mix(p, mu_prev, mu_next):
    prev = jnp.pad(p[:, :-1], ((0, 0), (1, 0), (0, 0)))
    nxt = jnp.pad(p[:, 1:], ((0, 0), (0, 1), (0, 0)))
    return p + mu_prev * (prev - p) + mu_next * (nxt - p)


def wkv7_scan(r, dec, k, v, kk, b):
    bsz, _, h, dh = r.shape
    xs = tuple(jnp.moveaxis(t, 1, 0) for t in (r, dec, k, v, kk, b))

    def step(S, inp):
        r_t, d_t, k_t, v_t, kk_t, b_t = inp
        sa = jnp.einsum("bhvk,bhk->bhv", S, kk_t)
        S = S * d_t[:, :, None, :] - sa[..., None] * b_t[:, :, None, :] + v_t[..., None] * k_t[:, :, None, :]
        return S, jnp.einsum("bhvk,bhk->bhv", S, r_t)

    _, ys = lax.scan(step, jnp.zeros((bsz, h, dh, dh), jnp.float32), xs)
    return jnp.moveaxis(ys, 0, 1)


def rwkv7_bidirectional(pl, pc, mu_prev, mu_next, w0, w2, a0, a2, g2, k_k, k_a, r_k, lnx_g, lnx_b, with_ctx):
    n_ctx = pc.shape[1]
    z = jnp.concatenate([shift_mix(pc, mu_prev, mu_next), shift_mix(pl, mu_prev, mu_next)],
                        axis=1).astype(jnp.float32)
    bsz, n, _ = z.shape

    def heads(t):
        return t.reshape(bsz, t.shape[1], RW_HEADS, RW_HEAD)

    def seg_flip(t):
        return jnp.concatenate([t[:, :n_ctx][:, ::-1], t[:, n_ctx:][:, ::-1]], axis=1)

    r = heads(z[..., RW_R:RW_K])
    k = z[..., RW_K:RW_V]
    v = heads(z[..., RW_V:RW_G])
    kk = heads(k * k_k)
    kk = kk / jnp.maximum(jnp.sqrt(jnp.sum(kk * kk, axis=-1, keepdims=True)), 1e-12)
    sel = slice(0, n) if with_ctx else slice(n_ctx, n)
    wkv, bonus = [], []
    for d in range(2):
        wl = z[..., RW_W + d * RW_DECAY_LORA: RW_W + (d + 1) * RW_DECAY_LORA]
        al = z[..., RW_A + d * RW_A_LORA: RW_A + (d + 1) * RW_A_LORA]
        log_w = -jax.nn.softplus(-(w0[d] + jnp.tanh(wl) @ w2[d])) - 0.5
        dec = heads(jnp.exp(-jnp.exp(log_w)))
        a = jax.nn.sigmoid(a0[d] + al @ a2[d])
        kd = heads(k * (1.0 + (a - 1.0) * k_a))
        ins = (r, dec, kd, v, kk, kk * heads(a))
        if d == 0:
            wkv.append(wkv7_scan(*ins))
        else:
            wkv.append(seg_flip(wkv7_scan(*[seg_flip(t) for t in ins])))
        bonus.append(jnp.sum(r[:, sel] * kd[:, sel] * r_k, axis=-1, keepdims=True) * v[:, sel])
    o = group_norm((wkv[0] + wkv[1])[:, sel].reshape(bsz, -1, GROUP_W), RW_HEADS, lnx_g, lnx_b, RW_LNX_EPS)
    gate = jax.nn.sigmoid(z[:, sel, RW_G:RW_W]) @ g2
    o = ((o + (bonus[0] + bonus[1]).reshape(bsz, -1, GROUP_W)) * gate).astype(pl.dtype)
    if with_ctx:
        return o[:, n_ctx:], o[:, :n_ctx]
    return o, None


def axial_angles(n):
    rows = n // GRID_W
    row = jnp.repeat(jnp.arange(rows), GRID_W).astype(jnp.float32)
    col = jnp.tile(jnp.arange(GRID_W), rows).astype(jnp.float32)
    n_freq = MLA_ROPE // 4
    freq = ROPE_BASE ** (-jnp.arange(n_freq, dtype=jnp.float32) / n_freq)
    return jnp.stack([row[:, None] * freq, col[:, None] * freq], axis=1)


def rope_2d(x, ang):
    xs = x.astype(jnp.float32).reshape(*x.shape[:-1], 2, 2, MLA_ROPE // 4)
    x1, x2 = xs[..., 0, :], xs[..., 1, :]
    cos, sin = jnp.cos(ang), jnp.sin(ang)
    y = jnp.stack([x1 * cos - x2 * sin, x1 * sin + x2 * cos], axis=-2)
    return y.reshape(x.shape).astype(x.dtype)


def mla_keys_values(pkv, kvn_g, wukv, ang):
    b, n, _ = pkv.shape
    c_kv = rms_norm(pkv[..., :MLA_KV_RANK], kvn_g)
    kv = (c_kv @ wukv.astype(c_kv.dtype)).reshape(b, n, MLA_HEADS, MLA_NOPE + MLA_V)
    k_rope = pkv[..., MLA_KV_RANK:]
    if ang is not None:
        k_rope = rope_2d(k_rope, ang)
    k = jnp.concatenate([kv[..., :MLA_NOPE],
                         jnp.broadcast_to(k_rope[:, :, None, :], (b, n, MLA_HEADS, MLA_ROPE))], axis=-1)
    return k, kv[..., MLA_NOPE:]


def mla_queries(pq, qn_g, wuq, ang):
    b, n, _ = pq.shape
    q = (rms_norm(pq, qn_g) @ wuq.astype(pq.dtype)).reshape(b, n, MLA_HEADS, MLA_NOPE + MLA_ROPE)
    q_rope = q[..., MLA_NOPE:]
    if ang is not None:
        q_rope = rope_2d(q_rope, ang[:, None])
    return jnp.concatenate([q[..., :MLA_NOPE], q_rope], axis=-1)


def block_attention(q, k, v):
    b, n, h, dq = q.shape
    nb = n // Q_BLOCK
    qb = jnp.moveaxis(q.reshape(b, nb, Q_BLOCK, h, dq), 1, 0)
    scale = dq ** -0.5

    def one_block(qi):
        s = jnp.einsum("bqhd,bkhd->bhqk", qi, k).astype(jnp.float32) * scale
        pr = jax.nn.softmax(s, axis=-1).astype(v.dtype)
        return jnp.einsum("bhqk,bkhd->bqhd", pr, v)

    o = lax.map(one_block, qb)
    return jnp.moveaxis(o, 0, 1).reshape(b, n, h * v.shape[-1])


def sq_relu_mlp(h, w1, w2):
    return jnp.square(jax.nn.relu(h @ w1)) @ w2


def setup_inputs(seed: int = 0) -> dict:
    key = jax.random.key(seed)
    ks = iter(jax.random.split(key, 40))

    def nrm(shape, s):
        return jax.random.normal(next(ks), shape, jnp.float32) * s

    def gain(shape):
        return 1.0 + nrm(shape, 0.1)

    L, D, G = DEPTH, D_MODEL, GROUP_W
    return {
        "x": nrm((BATCH, SEQ, D), 1.0),
        "c": nrm((BATCH, D), 1.0),
        "ctx": nrm((BATCH, CTX_LEN, D), 1.0),
        "c_ctx": nrm((D,), 1.0),
        "ada_w": nrm((L, D, 6 * D), D ** -0.5),
        "ada_b": nrm((L, 6 * D), 0.02),
        "norm1_g": gain((L, D)),
        "norm2_g": gain((L, D)),
        "w_in": nrm((L, D, P_IN), D ** -0.5),
        "w_out": nrm((L, D_MIX, D), D_MIX ** -0.5),
        "conv_dw": nrm((L, CONV_K, G), CONV_K ** -0.5),
        "conv_db": nrm((L, G), 0.01),
        "conv_gn_g": gain((L, G)),
        "conv_gn_b": nrm((L, G), 0.01),
        "conv_pw": nrm((L, G, G), G ** -0.5),
        "pool_w": nrm((L, len(POOL_WINDOWS), POOL_CH, POOL_CH), POOL_CH ** -0.5),
        "pool_scale": gain((L, G)),
        "rw_mu_prev": jax.random.uniform(next(ks), (L, N_RW_IN), jnp.float32, 0.0, 0.5),
        "rw_mu_next": jax.random.uniform(next(ks), (L, N_RW_IN), jnp.float32, 0.0, 0.5),
        "rw_w0": nrm((L, 2, G), 0.5),
        "rw_w2": nrm((L, 2, RW_DECAY_LORA, G), 0.5 * RW_DECAY_LORA ** -0.5),
        "rw_a0": nrm((L, 2, G), 0.1),
        "rw_a2": nrm((L, 2, RW_A_LORA, G), 0.5 * RW_A_LORA ** -0.5),
        "rw_g2": nrm((L, RW_GATE_LORA, G), RW_GATE_LORA ** -0.5),
        "rw_kk": gain((L, G)),
        "rw_ka": gain((L, G)),
        "rw_rk": nrm((L, RW_HEADS, RW_HEAD), 0.1),
        "rw_lnx_g": gain((L, G)),
        "rw_lnx_b": nrm((L, G), 0.01),
        "mla_qn_g": gain((L, MLA_Q_RANK)),
        "mla_wuq": nrm((L, MLA_Q_RANK, MLA_HEADS * (MLA_NOPE + MLA_ROPE)), MLA_Q_RANK ** -0.5),
        "mla_kvn_g": gain((L, MLA_KV_RANK)),
        "mla_wukv": nrm((L, MLA_KV_RANK, MLA_HEADS * (MLA_NOPE + MLA_V)), MLA_KV_RANK ** -0.5),
        "mlp_w1": nrm((L, D, D_FF), D ** -0.5),
        "mlp_w2": nrm((L, D_FF, D), D_FF ** -0.5),
        "final_g": gain((D,)),
    }


def reference(x, c, ctx, c_ctx, ada_w, ada_b, norm1_g, norm2_g, w_in, w_out, conv_dw, conv_db, conv_gn_g,
              conv_gn_b, conv_pw, pool_w, pool_scale, rw_mu_prev, rw_mu_next, rw_w0, rw_w2, rw_a0, rw_a2,
              rw_g2, rw_kk, rw_ka, rw_rk, rw_lnx_g, rw_lnx_b, mla_qn_g, mla_wuq, mla_kvn_g, mla_wukv,
              mlp_w1, mlp_w2, final_g):
    ang = axial_angles(x.shape[1])
    xc = ctx
    s_lat = jax.nn.silu(c)
    s_ctx = jax.nn.silu(c_ctx)
    for l in range(DEPTH):
        with_ctx = l < DEPTH - 1
        mod = jnp.split((s_lat @ ada_w[l] + ada_b[l])[:, None, :], 6, axis=-1)
        modc = jnp.split(s_ctx @ ada_w[l] + ada_b[l], 6, axis=-1)
        h = rms_norm(x, norm1_g[l]) * (1.0 + mod[1]) + mod[0]
        hc = rms_norm(xc, norm1_g[l]) * (1.0 + modc[1]) + modc[0]
        p = h @ w_in[l]
        pc_tail = hc @ w_in[l][:, OFF_RW:]

        y_conv = conv_module(p[..., :OFF_POOL], conv_dw[l], conv_db[l], conv_gn_g[l], conv_gn_b[l], conv_pw[l])
        y_pool = multiscale_pool(p[..., OFF_POOL:OFF_Q], pool_w[l], pool_scale[l])
        y_rw, yc_rw = rwkv7_bidirectional(p[..., OFF_RW:OFF_KV], pc_tail[..., :N_RW_IN], rw_mu_prev[l],
                                          rw_mu_next[l], rw_w0[l], rw_w2[l], rw_a0[l], rw_a2[l], rw_g2[l],
                                          rw_kk[l], rw_ka[l], rw_rk[l], rw_lnx_g[l], rw_lnx_b[l], with_ctx)
        k_l, v_l = mla_keys_values(p[..., OFF_KV:], mla_kvn_g[l], mla_wukv[l], ang)
        k_c, v_c = mla_keys_values(pc_tail[..., N_RW_IN:], mla_kvn_g[l], mla_wukv[l], None)
        q_l = mla_queries(p[..., OFF_Q:OFF_RW], mla_qn_g[l], mla_wuq[l], ang)
        y_att = block_attention(q_l, jnp.concatenate([k_c, k_l], axis=1), jnp.concatenate([v_c, v_l], axis=1))

        x = x + mod[2] * (jnp.concatenate([y_conv, y_rw, y_pool, y_att], axis=-1) @ w_out[l])
        x = x + mod[5] * sq_relu_mlp(rms_norm(x, norm2_g[l]) * (1.0 + mod[4]) + mod[3], mlp_w1[l], mlp_w2[l])

        if with_ctx:
            pc_head = hc @ w_in[l][:, :OFF_RW]
            yc_conv = conv_module(pc_head[..., :OFF_POOL], conv_dw[l], conv_db[l], conv_gn_g[l], conv_gn_b[l],
                                  conv_pw[l])
            yc_pool = multiscale_pool(pc_head[..., OFF_POOL:OFF_Q], pool_w[l], pool_scale[l])
            q_c = mla_queries(pc_head[..., OFF_Q:OFF_RW], mla_qn_g[l], mla_wuq[l], None)
            yc_att = block_attention(q_c, k_c, v_c)
            xc = xc + modc[2] * (jnp.concatenate([yc_conv, yc_rw, yc_pool, yc_att], axis=-1) @ w_out[l])
            xc = xc + modc[5] * sq_relu_mlp(rms_norm(xc, norm2_g[l]) * (1.0 + modc[4]) + modc[3],
                                            mlp_w1[l], mlp_w2[l])
    return rms_norm(x, final_g)
```

```python
import functools

import numpy as np
import jax
import jax.numpy as jnp
from jax import lax
from jax.experimental import pallas as pl
from jax.experimental.pallas import tpu as pltpu

f32 = jnp.float32
bf16 = jnp.bfloat16
HIGHEST = lax.Precision.HIGHEST

GROUP_W = 256
NORM_EPS = 1e-6
GN_EPS = 1e-5
CONV_K = 31
CONV_NORM_GROUPS = 4
RW_HEAD = 64
RW_HEADS = GROUP_W // RW_HEAD
RW_DECAY_LORA = 64
RW_A_LORA = 64
RW_GATE_LORA = 128
RW_LNX_EPS = 64e-5
POOL_WINDOWS = (2, 4, 8, 16)
MLA_HEADS = 4
MLA_NOPE = 64
MLA_ROPE = 32
MLA_V = 64
MLA_KV_RANK = 128
ROPE_BASE = 10000.0
GRID_W = 64
N_CONV_IN = 2 * GROUP_W
N_RW_IN = 3 * GROUP_W + RW_GATE_LORA + 2 * RW_DECAY_LORA + 2 * RW_A_LORA
N_KV_IN = MLA_KV_RANK + MLA_ROPE
OFF_POOL = N_CONV_IN
OFF_Q = OFF_POOL + GROUP_W
OFF_RW = OFF_Q + GROUP_W
OFF_KV = OFF_RW + N_RW_IN
P_IN = OFF_KV + N_KV_IN

LANES = 128
SUBLANES = 8
VMEM_LIMIT_BYTES = 56 * 1024 * 1024

RW_CHUNK = 64
HEAD_PAD = 128


def _cparams(sem):
    return pltpu.CompilerParams(dimension_semantics=sem, vmem_limit_bytes=VMEM_LIMIT_BYTES)


def _dot(a, b):
    return jnp.dot(a.astype(bf16), b.astype(bf16), preferred_element_type=f32)


def _dot_hi(a, b):
    return jnp.dot(a, b, preferred_element_type=f32, precision=HIGHEST)


def _dot_nt(a, b):
    return lax.dot_general(a.astype(bf16), b.astype(bf16), (((1,), (1,)), ((), ())), preferred_element_type=f32)


def _dot_tn(a, b):
    return lax.dot_general(a.astype(bf16), b.astype(bf16), (((0,), (0,)), ((), ())), preferred_element_type=f32)


def _sigmoid(x):
    return jax.nn.sigmoid(x)


def _mod_row(m_ref, j, is_ctx):
    lat = m_ref[0, 6 + j:7 + j, :]
    if is_ctx is None:
        return lat
    return jnp.where(is_ctx, m_ref[0, j:j + 1, :], lat)


def _is_ctx_rows(tile_idx, tm, n_ctx):
    if n_ctx == 0:
        return None
    row = tile_idx * tm + lax.broadcasted_iota(jnp.int32, (tm, 1), 0)
    return row < n_ctx


def _rms(x, g):
    return x * lax.rsqrt(jnp.mean(x * x, axis=-1, keepdims=True) + NORM_EPS) * g


def _ada_kernel(s_ref, w_ref, b_ref, o_ref):
    s = s_ref[...]
    s = s * _sigmoid(s)
    o_ref[...] = _dot_hi(s, w_ref[...]) + b_ref[...]


def _ada_call(s_all, w, b):
    rows, d = s_all.shape
    n = w.shape[1]
    tn = 1536
    return pl.pallas_call(
        _ada_kernel,
        grid=(n // tn,),
        in_specs=[pl.BlockSpec((rows, d), lambda j: (0, 0)),
                  pl.BlockSpec((d, tn), lambda j: (0, j)),
                  pl.BlockSpec((1, tn), lambda j: (0, j))],
        out_specs=pl.BlockSpec((rows, tn), lambda j: (0, j)),
        out_shape=jax.ShapeDtypeStruct((rows, n), f32),
        compiler_params=_cparams(("arbitrary",)),
        name="ada",
    )(s_all, w, b.reshape(1, n))


def _inproj_kernel(x_ref, m_ref, g_ref, w_ref, oc_ref, op_ref, oq_ref, orw_ref, okv_ref, *, n_ctx, tm):
    is_ctx = _is_ctx_rows(pl.program_id(1), tm, n_ctx)
    h = _rms(x_ref[0], g_ref[...]) * (1.0 + _mod_row(m_ref, 1, is_ctx)) + _mod_row(m_ref, 0, is_ctx)
    p = jnp.dot(h.astype(bf16), w_ref[...], preferred_element_type=f32)
    oc_ref[0] = p[:, 0:OFF_POOL]
    op_ref[0] = p[:, OFF_POOL:OFF_Q]
    oq_ref[0] = p[:, OFF_Q:OFF_RW]
    orw_ref[0] = p[:, OFF_RW:OFF_KV]
    okv_ref[0] = p[:, OFF_KV:P_IN]


def _inproj_call(xx, modall, g, w_pad, n_ctx):
    bsz, n, d = xx.shape
    tm = 384 if n % 384 == 0 else 256
    widths = (N_CONV_IN, GROUP_W, GROUP_W, N_RW_IN, N_KV_IN)
    return pl.pallas_call(
        functools.partial(_inproj_kernel, n_ctx=n_ctx, tm=tm),
        grid=(bsz, n // tm),
        in_specs=[pl.BlockSpec((1, tm, d), lambda b, i: (b, i, 0)),
                  pl.BlockSpec((1, 12, d), lambda b, i: (b, 0, 0)),
                  pl.BlockSpec((1, d), lambda b, i: (0, 0)),
                  pl.BlockSpec(w_pad.shape, lambda b, i: (0, 0))],
        out_specs=[pl.BlockSpec((1, tm, w), lambda b, i: (b, i, 0)) for w in widths],
        out_shape=[jax.ShapeDtypeStruct((bsz, n, w), f32) for w in widths],
        compiler_params=_cparams(("parallel", "parallel")),
        name="inproj",
    )(xx, modall, g.reshape(1, d), w_pad)


CONV_ROWS = 128
CONV_HALO = 16


def _conv_kernel(p_ref, dw_ref, db_ref, gg_ref, gb_ref, pw_ref, gavg_ref, o_ref, u_scr, *, segs):
    r, hl = CONV_ROWS, CONV_HALO
    win_rows = r + 2 * hl
    for s0, n in segs:
        u_scr[0:hl, :] = jnp.zeros((hl, GROUP_W), f32)
        u_scr[hl + n:hl + n + hl, :] = jnp.zeros((hl, GROUP_W), f32)

        def fill(c, carry, s0=s0):
            r0 = pl.multiple_of(c * r, r)
            blk = p_ref[0, pl.ds(s0 + r0, r), :]
            u_scr[pl.ds(hl + r0, r), :] = blk[:, :GROUP_W] * _sigmoid(blk[:, GROUP_W:])
            return carry

        lax.fori_loop(0, n // r, fill, 0)

        def body(c, carry, s0=s0):
            r0 = pl.multiple_of(c * r, r)
            win = u_scr[pl.ds(r0, win_rows), :]
            acc = jnp.zeros((r, GROUP_W), f32)
            for j in range(CONV_K):
                off = hl - CONV_K // 2 + j
                tap = pltpu.roll(win, (win_rows - off) % win_rows, axis=0)[0:r]
                acc = acc + tap * dw_ref[j:j + 1, :]
            y = acc + db_ref[...]
            mu = _dot_hi(y, gavg_ref[...])
            dlt = y - mu
            var = _dot_hi(dlt * dlt, gavg_ref[...])
            yn = dlt * lax.rsqrt(var + GN_EPS) * gg_ref[...] + gb_ref[...]
            act = yn * _sigmoid(yn)
            o_ref[0, pl.ds(s0 + r0, r), :] = _dot(act, pw_ref[...]).astype(bf16)
            return carry

        lax.fori_loop(0, n // r, body, 0)


def _group_avg_matrix(width, group):
    idx = np.arange(width) // group
    return jnp.asarray((idx[:, None] == idx[None, :]).astype(np.float32) / group)


def _conv_call(p_conv, dw, db, gg, gb, pw, segs):
    bsz, n, _ = p_conv.shape
    gavg = _group_avg_matrix(GROUP_W, GROUP_W // CONV_NORM_GROUPS)
    max_seg = max(s[1] for s in segs)
    vec = lambda: pl.BlockSpec((1, GROUP_W), lambda b: (0, 0))
    return pl.pallas_call(
        functools.partial(_conv_kernel, segs=segs),
        grid=(bsz,),
        in_specs=[pl.BlockSpec((1, n, N_CONV_IN), lambda b: (b, 0, 0)),
                  pl.BlockSpec((CONV_K, GROUP_W), lambda b: (0, 0)),
                  vec(), vec(), vec(),
                  pl.BlockSpec((GROUP_W, GROUP_W), lambda b: (0, 0)),
                  pl.BlockSpec((GROUP_W, GROUP_W), lambda b: (0, 0))],
        out_specs=pl.BlockSpec((1, n, GROUP_W), lambda b: (b, 0, 0)),
        out_shape=jax.ShapeDtypeStruct((bsz, n, GROUP_W), bf16),
        scratch_shapes=[pltpu.VMEM((max_seg + 2 * CONV_HALO, GROUP_W), f32)],
        compiler_params=_cparams(("parallel",)),
        name="conv",
    )(p_conv, dw, db.reshape(1, -1), gg.reshape(1, -1), gb.reshape(1, -1), pw.astype(bf16), gavg)


def _pool_kernel(p_ref, w_ref, sc_ref, o_ref, u_scr, *, segs):
    r, hl = CONV_ROWS, CONV_HALO
    win_rows = r + 2 * hl
    pool_ch = GROUP_W // len(POOL_WINDOWS)
    lane = lax.broadcasted_iota(jnp.int32, (1, GROUP_W), 1)
    half = jnp.full((1, GROUP_W), POOL_WINDOWS[-1] // 2, jnp.int32)
    for gi in range(len(POOL_WINDOWS) - 2, -1, -1):
        half = jnp.where(lane < (gi + 1) * pool_ch, POOL_WINDOWS[gi] // 2, half)

    def shifted(v, k):
        return pltpu.roll(v, (win_rows - k) % win_rows, axis=0)

    for s0, n in segs:
        u_scr[0:hl, :] = jnp.zeros((hl, GROUP_W), f32)
        u_scr[hl + n:hl + n + hl, :] = jnp.zeros((hl, GROUP_W), f32)

        def fill(c, carry, s0=s0):
            r0 = pl.multiple_of(c * r, r)
            u_scr[pl.ds(hl + r0, r), :] = p_ref[0, pl.ds(s0 + r0, r), :]
            return carry

        lax.fori_loop(0, n // r, fill, 0)

        def body(c, carry, s0=s0, n=n):
            r0 = pl.multiple_of(c * r, r)
            win = u_scr[pl.ds(r0, win_rows), :]
            s2 = win + shifted(win, -1)
            s4 = shifted(s2, -1) + shifted(s2, 1)
            s8 = shifted(s4, -2) + shifted(s4, 2)
            s16 = shifted(s8, -4) + shifted(s8, 4)
            sums = (s2, s4, s8, s16)
            sel = sums[-1]
            for gi in range(len(POOL_WINDOWS) - 2, -1, -1):
                sel = jnp.where(lane < (gi + 1) * pool_ch, sums[gi], sel)
            sel = sel[hl:hl + r]
            u = win[hl:hl + r]
            t = r0 + lax.broadcasted_iota(jnp.int32, (r, 1), 0)
            cnt = (jnp.minimum(t + half, n) - jnp.maximum(t - half, 0)).astype(f32)
            dlt = sel / cnt - u
            o_ref[0, pl.ds(s0 + r0, r), :] = (_dot(dlt, w_ref[...]) * sc_ref[...]).astype(bf16)
            return carry

        lax.fori_loop(0, n // r, body, 0)


def _block_diag(blocks):
    g, a, b = blocks.shape
    out = jnp.zeros((g * a, g * b), blocks.dtype)
    for i in range(g):
        out = out.at[i * a:(i + 1) * a, i * b:(i + 1) * b].set(blocks[i])
    return out


def _pool_call(p_pool, pool_w, pool_scale, segs):
    bsz, n, _ = p_pool.shape
    max_seg = max(s[1] for s in segs)
    return pl.pallas_call(
        functools.partial(_pool_kernel, segs=segs),
        grid=(bsz,),
        in_specs=[pl.BlockSpec((1, n, GROUP_W), lambda b: (b, 0, 0)),
                  pl.BlockSpec((GROUP_W, GROUP_W), lambda b: (0, 0)),
                  pl.BlockSpec((1, GROUP_W), lambda b: (0, 0))],
        out_specs=pl.BlockSpec((1, n, GROUP_W), lambda b: (b, 0, 0)),
        out_shape=jax.ShapeDtypeStruct((bsz, n, GROUP_W), bf16),
        scratch_shapes=[pltpu.VMEM((max_seg + 2 * CONV_HALO, GROUP_W), f32)],
        compiler_params=_cparams(("parallel",)),
        name="pool",
    )(p_pool, _block_diag(pool_w).astype(bf16), pool_scale.reshape(1, -1))


RW_TILE = 256
RW_R, RW_K, RW_V, RW_G = 0, GROUP_W, 2 * GROUP_W, 3 * GROUP_W
RW_W = RW_G + RW_GATE_LORA
RW_A = RW_W + 2 * RW_DECAY_LORA


def _rwprep_kernel(p_ref, hp_ref, hn_ref, mup_ref, mun_ref, w0_ref, w2_ref, a0_ref, a2_ref, g2_ref,
                   kk_ref, ka_ref, rk_ref, tril_ref, triu_ref, hsum_ref,
                   rt0, kq0, bt0, kt0, pc0, rt1, kq1, bt1, kt1, pc1, v_out, bonus_out, gate_out,
                   *, seg_starts, seg_ends):
    tr, c = RW_TILE, RW_CHUNK
    i = pl.program_id(1)
    row0 = i * tr
    first = functools.reduce(jnp.logical_or, [row0 == s for s in seg_starts])
    last = functools.reduce(jnp.logical_or, [row0 + tr == e for e in seg_ends])
    ridx = lax.broadcasted_iota(jnp.int32, (tr, 1), 0)

    def zcols(a, b):
        p = p_ref[0, :, a:b]
        prev_row = jnp.where(first, 0.0, hp_ref[0, SUBLANES - 1:SUBLANES, a:b])
        next_row = jnp.where(last, 0.0, hn_ref[0, 0:1, a:b])
        prev = jnp.where(ridx == 0, prev_row, pltpu.roll(p, 1, axis=0))
        nxt = jnp.where(ridx == tr - 1, next_row, pltpu.roll(p, tr - 1, axis=0))
        return p + mup_ref[:, a:b] * (prev - p) + mun_ref[:, a:b] * (nxt - p)

    r = zcols(RW_R, RW_K)
    k = zcols(RW_K, RW_V)
    v = zcols(RW_V, RW_G)
    v_out[0] = v
    gate_out[0] = _dot(_sigmoid(zcols(RW_G, RW_W)), g2_ref[...])
    kk = k * kk_ref[...]
    kk = kk / jnp.maximum(jnp.sqrt(_dot_hi(kk * kk, hsum_ref[...])), 1e-12)
    w_all = _dot_hi(jnp.tanh(zcols(RW_W, RW_A)), w2_ref[...]) + w0_ref[...]
    a_all = _sigmoid(_dot_hi(zcols(RW_A, N_RW_IN), a2_ref[...]) + a0_ref[...])
    kd_sum = jnp.zeros_like(k)
    outs = ((rt0, kq0, bt0, kt0, pc0, tril_ref), (rt1, kq1, bt1, kt1, pc1, triu_ref))
    for d, (rt_o, kq_o, bt_o, kt_o, pc_o, tri_ref) in enumerate(outs):
        x = w_all[:, d * GROUP_W:(d + 1) * GROUP_W]
        neg = -x
        log_w = -(jnp.maximum(neg, 0.0) + jnp.log1p(jnp.exp(-jnp.abs(neg)))) - 0.5
        lw = -jnp.exp(log_w)
        a = a_all[:, d * GROUP_W:(d + 1) * GROUP_W]
        kd = k * (1.0 + (a - 1.0) * ka_ref[...])
        kd_sum = kd_sum + kd
        b = kk * a
        cum = _dot_hi(tri_ref[...], lw)
        tot_rows = []
        for ci in range(tr // c):
            edge = ci * c + (c - 1 if d == 0 else 0)
            tot_rows.append(jnp.broadcast_to(cum[edge:edge + 1, :], (c, GROUP_W)))
        tot = jnp.concatenate(tot_rows, axis=0)
        e_neg = jnp.exp(-cum)
        rt_o[0] = r * jnp.exp(cum)
        kq_o[0] = kk * jnp.exp(cum - lw)
        bt_o[0] = b * e_neg
        kt_o[0] = kd * e_neg
        pc_o[0] = jnp.exp(tot)
    bonus_out[0] = _dot_hi(r * rk_ref[...] * kd_sum, hsum_ref[...]) * v


def _chunk_tri(tile, chunk, upper):
    t = np.arange(tile)
    same = (t[:, None] // chunk) == (t[None, :] // chunk)
    tri = (t[None, :] >= t[:, None]) if upper else (t[None, :] <= t[:, None])
    return jnp.asarray((same & tri).astype(np.float32))


def _rwprep_call(p_rw, mu_prev, mu_next, w0, w2, a0, a2, g2, kkp, kap, rk, n_ctx):
    bsz, n, _ = p_rw.shape
    tr = RW_TILE
    nb8 = n // SUBLANES
    seg_starts = tuple(sorted({0, n_ctx}))
    seg_ends = tuple(sorted({n_ctx, n} - {0}))
    zeros = jnp.zeros((RW_DECAY_LORA, GROUP_W), f32)
    w2cat = jnp.concatenate([jnp.concatenate([w2[0], zeros], 1), jnp.concatenate([zeros, w2[1]], 1)], 0)
    a2cat = jnp.concatenate([jnp.concatenate([a2[0], zeros], 1), jnp.concatenate([zeros, a2[1]], 1)], 0)
    hsum = _group_avg_matrix(GROUP_W, RW_HEAD) * RW_HEAD
    full = lambda shape: pl.BlockSpec(shape, lambda b, i: tuple(0 for _ in shape))
    out_spec = pl.BlockSpec((1, tr, GROUP_W), lambda b, i: (b, i, 0))
    return pl.pallas_call(
        functools.partial(_rwprep_kernel, seg_starts=seg_starts, seg_ends=seg_ends),
        grid=(bsz, n // tr),
        in_specs=[pl.BlockSpec((1, tr, N_RW_IN), lambda b, i: (b, i, 0)),
                  pl.BlockSpec((1, SUBLANES, N_RW_IN),
                               lambda b, i: (b, jnp.maximum(i * (tr // SUBLANES) - 1, 0), 0)),
                  pl.BlockSpec((1, SUBLANES, N_RW_IN),
                               lambda b, i: (b, jnp.minimum((i + 1) * (tr // SUBLANES), nb8 - 1), 0)),
                  full((1, N_RW_IN)), full((1, N_RW_IN)),
                  full((1, 2 * GROUP_W)), full((2 * RW_DECAY_LORA, 2 * GROUP_W)),
                  full((1, 2 * GROUP_W)), full((2 * RW_A_LORA, 2 * GROUP_W)),
                  full((RW_GATE_LORA, GROUP_W)),
                  full((1, GROUP_W)), full((1, GROUP_W)), full((1, GROUP_W)),
                  full((tr, tr)), full((tr, tr)), full((GROUP_W, GROUP_W))],
        out_specs=[out_spec] * 13,
        out_shape=[jax.ShapeDtypeStruct((bsz, n, GROUP_W), f32)] * 13,
        compiler_params=_cparams(("parallel", "parallel")),
        name="rwprep",
    )(p_rw, p_rw, p_rw, mu_prev.reshape(1, -1), mu_next.reshape(1, -1),
      w0.reshape(1, -1), w2cat, a0.reshape(1, -1), a2cat, g2.astype(bf16),
      kkp.reshape(1, -1), kap.reshape(1, -1), rk.reshape(1, -1),
      _chunk_tri(tr, RW_CHUNK, False), _chunk_tri(tr, RW_CHUNK, True), hsum)


def _rwscan_kernel(rt0, kq0, bt0, kt0, pc0, v0, rt1, kq1, bt1, kt1, pc1, v1, hm_ref,
                   y0_ref, y1_ref, s0_scr, s1_scr):
    c = RW_CHUNK
    hc = RW_HEADS * c

    @pl.when(pl.program_id(1) == 0)
    def _():
        s0_scr[...] = jnp.zeros_like(s0_scr)
        s1_scr[...] = jnp.zeros_like(s1_scr)

    hm = hm_ref[...]
    ri = lax.broadcasted_iota(jnp.int32, (hc, hc), 0)
    ci = lax.broadcasted_iota(jnp.int32, (hc, hc), 1)
    eye = (ri == ci).astype(f32)

    def stack(x):
        return jnp.concatenate([x] * RW_HEADS, axis=0) * hm

    dirs = ((rt0, kq0, bt0, kt0, pc0, v0, y0_ref, s0_scr, ri > ci, ri >= ci),
            (rt1, kq1, bt1, kt1, pc1, v1, y1_ref, s1_scr, ri < ci, ri <= ci))
    for rt_r, kq_r, bt_r, kt_r, pc_r, v_r, y_ref, s_scr, strict, incl in dirs:
        pc = pc_r[0]
        rs = stack(rt_r[0]).astype(bf16)
        ks = stack(kq_r[0]).astype(bf16)
        bs = stack(bt_r[0]).astype(bf16)
        kts = stack(kt_r[0]).astype(bf16)
        vs = stack(v_r[0]).astype(bf16)
        bps = stack(bt_r[0] * pc).astype(bf16)
        kps = stack(kt_r[0] * pc).astype(bf16)
        a_ub = jnp.where(strict, _dot_nt(ks, bs), 0.0)
        a_vk = jnp.where(strict, _dot_nt(ks, kts), 0.0)
        a_rb = jnp.where(incl, _dot_nt(rs, bs), 0.0)
        a_rk = jnp.where(incl, _dot_nt(rs, kts), 0.0)
        tinv = eye - a_ub
        apow = a_ub
        for _ in range(int(np.log2(c)) - 1):
            apow = _dot(apow, apow)
            tinv = tinv + _dot(tinv, apow)
        khat = _dot(tinv, ks)
        w2 = _dot(tinv, _dot(a_vk, vs))
        s = s_scr[...]
        us_t = -(_dot_nt(s, khat) + w2.T)
        us = us_t.T
        ys = _dot_nt(rs, s) + _dot(a_rb, us) + _dot(a_rk, vs)
        y = ys[0:c]
        for h in range(1, RW_HEADS):
            y = y + ys[h * c:(h + 1) * c]
        y_ref[0] = y
        pc_rows = jnp.concatenate([pc] * RW_HEADS, axis=0)
        s_scr[...] = s * pc_rows + _dot(us_t, bps) + _dot_tn(vs, kps)


def _rwscan_call(prep, n_ctx):
    rt0, kq0, bt0, kt0, pc0, rt1, kq1, bt1, kt1, pc1, v = prep
    bsz, n, _ = v.shape
    c = RW_CHUNK
    n_chunks = n // c
    ctx_chunks = n_ctx // c
    hc = RW_HEADS * c
    head_mask = jnp.asarray(
        (np.arange(hc)[:, None] // c == np.arange(GROUP_W)[None, :] // RW_HEAD).astype(np.float32))

    def fwd(b, i):
        return (b, i, 0)

    def bwd(b, i):
        return (b, jnp.where(i < ctx_chunks, ctx_chunks - 1 - i, n_chunks - 1 - i + ctx_chunks), 0)

    blk = (1, c, GROUP_W)
    return pl.pallas_call(
        _rwscan_kernel,
        grid=(bsz, n_chunks),
        in_specs=[pl.BlockSpec(blk, fwd)] * 6 + [pl.BlockSpec(blk, bwd)] * 6
                 + [pl.BlockSpec((hc, GROUP_W), lambda b, i: (0, 0))],
        out_specs=[pl.BlockSpec(blk, fwd), pl.BlockSpec(blk, bwd)],
        out_shape=[jax.ShapeDtypeStruct((bsz, n, GROUP_W), f32)] * 2,
        scratch_shapes=[pltpu.VMEM((GROUP_W, GROUP_W), f32)] * 2,
        compiler_params=_cparams(("parallel", "arbitrary")),
        name="rwscan",
    )(rt0, kq0, bt0, kt0, pc0, v, rt1, kq1, bt1, kt1, pc1, v, head_mask)


def _rwpost_kernel(y0_ref, y1_ref, bonus_ref, gate_ref, g_ref, b_ref, gavg_ref, o_ref):
    y = y0_ref[0] + y1_ref[0]
    mu = _dot_hi(y, gavg_ref[...])
    dlt = y - mu
    var = _dot_hi(dlt * dlt, gavg_ref[...])
    o = dlt * lax.rsqrt(var + RW_LNX_EPS) * g_ref[...] + b_ref[...]
    o_ref[0] = ((o + bonus_ref[0]) * gate_ref[0]).astype(bf16)


def _rwpost_call(y0, y1, bonus, gate, g, b):
    bsz, n, _ = y0.shape
    tr = RW_TILE
    spec = pl.BlockSpec((1, tr, GROUP_W), lambda bb, i: (bb, i, 0))
    vec = pl.BlockSpec((1, GROUP_W), lambda bb, i: (0, 0))
    return pl.pallas_call(
        _rwpost_kernel,
        grid=(bsz, n // tr),
        in_specs=[spec] * 4 + [vec, vec, pl.BlockSpec((GROUP_W, GROUP_W), lambda bb, i: (0, 0))],
        out_specs=spec,
        out_shape=jax.ShapeDtypeStruct((bsz, n, GROUP_W), bf16),
        compiler_params=_cparams(("parallel", "parallel")),
        name="rwpost",
    )(y0, y1, bonus, gate, g.reshape(1, -1), b.reshape(1, -1), _group_avg_matrix(GROUP_W, RW_HEAD))


ATT_TQ = 256
ATT_KV_ROWS = 256


def _mla_kernel(pq_ref, pkv_ref, cos_ref, sin_ref, qg_ref, wqa_ref, wqb_ref, kvg_ref, wk_ref, wv_ref,
                e1_ref, e2_ref, o_ref, k_scr, v_scr, *, n_ctx, n_all):
    qi = pl.program_id(1)
    tq = ATT_TQ
    hp = HEAD_PAD

    def tile4(t):
        return jnp.concatenate([t] * MLA_HEADS, axis=1)

    @pl.when(qi == 0)
    def _():
        def build(c, carry):
            r0 = pl.multiple_of(c * ATT_KV_ROWS, ATT_KV_ROWS)
            pkv = pkv_ref[0, pl.ds(r0, ATT_KV_ROWS), :]
            ckv = _rms(pkv[:, :MLA_KV_RANK], kvg_ref[...])
            kr = pkv[:, MLA_KV_RANK:]
            cos = tile4(cos_ref[pl.ds(r0, ATT_KV_ROWS), :])
            sin = tile4(sin_ref[pl.ds(r0, ATT_KV_ROWS), :])
            kmat = _dot(ckv, wk_ref[...]) + _dot_hi(kr, e1_ref[...]) * cos + _dot_hi(kr, e2_ref[...]) * sin
            k_scr[pl.ds(r0, ATT_KV_ROWS), :] = kmat.astype(bf16)
            v_scr[pl.ds(r0, ATT_KV_ROWS), :] = _dot(ckv, wv_ref[...]).astype(bf16)
            return carry

        lax.fori_loop(0, n_all // ATT_KV_ROWS, build, 0)

    r0 = pl.multiple_of(qi * tq, tq)
    qn = _rms(pq_ref[0], qg_ref[...]).astype(bf16)
    cos = tile4(cos_ref[pl.ds(r0, tq), :])
    sin = tile4(sin_ref[pl.ds(r0, tq), :])
    scale = float(MLA_NOPE + MLA_ROPE) ** -0.5
    q = (jnp.dot(qn, wqa_ref[...], preferred_element_type=f32) * cos
         + jnp.dot(qn, wqb_ref[...], preferred_element_type=f32) * sin) * scale
    q = q.astype(bf16)

    def attend(n_keys):
        for pair in range(MLA_HEADS // 2):
            acc = jnp.zeros((tq, hp), f32)
            for h in (2 * pair, 2 * pair + 1):
                s = _dot_nt(q[:, h * hp:(h + 1) * hp], k_scr[0:n_keys, h * hp:(h + 1) * hp])
                e = jnp.exp(s - jnp.max(s, axis=-1, keepdims=True))
                inv = 1.0 / jnp.sum(e, axis=-1, keepdims=True)
                acc = acc + _dot(e, v_scr[0:n_keys, h * hp:(h + 1) * hp]) * inv
            o_ref[0, :, pair * hp:(pair + 1) * hp] = acc.astype(bf16)

    if n_ctx > 0:
        @pl.when(qi < n_ctx // tq)
        def _():
            attend(n_ctx)

        @pl.when(qi >= n_ctx // tq)
        def _():
            attend(n_all)
    else:
        attend(n_all)


def _rope_rotation():
    quarter = MLA_ROPE // 4
    rot = np.zeros((MLA_ROPE, MLA_ROPE), np.float32)
    for axis in range(2):
        for f in range(quarter):
            first, second = axis * 2 * quarter + f, axis * 2 * quarter + quarter + f
            rot[second, first] = -1.0
            rot[first, second] = 1.0
    return rot


def _rope_tables(n_ctx, n_lat):
    rows = n_lat // GRID_W
    row = jnp.repeat(jnp.arange(rows), GRID_W).astype(f32)
    col = jnp.tile(jnp.arange(GRID_W), rows).astype(f32)
    n_freq = MLA_ROPE // 4
    freq = ROPE_BASE ** (-jnp.arange(n_freq, dtype=f32) / n_freq)
    ang = jnp.concatenate([row[:, None] * freq, row[:, None] * freq, col[:, None] * freq, col[:, None] * freq], 1)
    cos = jnp.ones((n_ctx + n_lat, HEAD_PAD), f32).at[n_ctx:, MLA_NOPE:MLA_NOPE + MLA_ROPE].set(jnp.cos(ang))
    sin = jnp.zeros((n_ctx + n_lat, HEAD_PAD), f32).at[n_ctx:, MLA_NOPE:MLA_NOPE + MLA_ROPE].set(jnp.sin(ang))
    return cos, sin


def _mla_weights(wuq, wukv):
    rot = _rope_rotation()
    src = np.argmax(np.abs(rot), axis=0)
    sign = jnp.asarray(rot[src, np.arange(MLA_ROPE)])
    hq = MLA_NOPE + MLA_ROPE
    hkv = MLA_NOPE + MLA_V
    wqa = jnp.zeros((wuq.shape[0], MLA_HEADS * HEAD_PAD), f32)
    wqb = jnp.zeros_like(wqa)
    wk = jnp.zeros((MLA_KV_RANK, MLA_HEADS * HEAD_PAD), f32)
    wv = jnp.zeros_like(wk)
    e1 = np.zeros((MLA_ROPE, MLA_HEADS * HEAD_PAD), np.float32)
    for h in range(MLA_HEADS):
        c0 = h * HEAD_PAD
        wqa = wqa.at[:, c0:c0 + hq].set(wuq[:, h * hq:(h + 1) * hq])
        wqb = wqb.at[:, c0 + MLA_NOPE:c0 + hq].set(wuq[:, h * hq + MLA_NOPE:(h + 1) * hq][:, src] * sign)
        wk = wk.at[:, c0:c0 + MLA_NOPE].set(wukv[:, h * hkv:h * hkv + MLA_NOPE])
        v0 = c0 + (h % 2) * MLA_V
        wv = wv.at[:, v0:v0 + MLA_V].set(wukv[:, h * hkv + MLA_NOPE:(h + 1) * hkv])
        e1[np.arange(MLA_ROPE), c0 + MLA_NOPE + np.arange(MLA_ROPE)] = 1.0
    return wqa.astype(bf16), wqb.astype(bf16), wk.astype(bf16), wv.astype(bf16), jnp.asarray(e1), jnp.asarray(rot @ e1)


def _mla_call(p_q, p_kv, cos, sin, qn_g, wuq, kvn_g, wukv, n_ctx):
    bsz, n, q_rank = p_q.shape
    wqa, wqb, wk, wv, e1, e2 = _mla_weights(wuq, wukv)
    hw = MLA_HEADS * HEAD_PAD
    full = lambda shape: pl.BlockSpec(shape, lambda b, i: tuple(0 for _ in shape))
    return pl.pallas_call(
        functools.partial(_mla_kernel, n_ctx=n_ctx, n_all=n),
        grid=(bsz, n // ATT_TQ),
        in_specs=[pl.BlockSpec((1, ATT_TQ, q_rank), lambda b, i: (b, i, 0)),
                  pl.BlockSpec((1, n, N_KV_IN), lambda b, i: (b, 0, 0)),
                  full((n, HEAD_PAD)), full((n, HEAD_PAD)),
                  full((1, q_rank)), full((q_rank, hw)), full((q_rank, hw)),
                  full((1, MLA_KV_RANK)), full((MLA_KV_RANK, hw)), full((MLA_KV_RANK, hw)),
                  full((MLA_ROPE, hw)), full((MLA_ROPE, hw))],
        out_specs=pl.BlockSpec((1, ATT_TQ, GROUP_W), lambda b, i: (b, i, 0)),
        out_shape=jax.ShapeDtypeStruct((bsz, n, GROUP_W), bf16),
        scratch_shapes=[pltpu.VMEM((n, hw), bf16), pltpu.VMEM((n, hw), bf16)],
        compiler_params=_cparams(("parallel", "arbitrary")),
        name="mla",
    )(p_q, p_kv, cos, sin, qn_g.reshape(1, -1), wqa, wqb, kvn_g.reshape(1, -1), wk, wv, e1, e2)


OUT_TM = 256


def _outproj_kernel(yc_ref, yr_ref, yp_ref, ya_ref, x_ref, m_ref, g_ref, w_ref, x1_ref, h2_ref, *, n_ctx, tile0):
    is_ctx = _is_ctx_rows(pl.program_id(1) + tile0, OUT_TM, n_ctx)
    y = jnp.concatenate([yc_ref[0], yr_ref[0], yp_ref[0], ya_ref[0]], axis=1)
    x1 = x_ref[0] + _mod_row(m_ref, 2, is_ctx) * jnp.dot(y, w_ref[...], preferred_element_type=f32)
    x1_ref[0] = x1
    h2 = _rms(x1, g_ref[...]) * (1.0 + _mod_row(m_ref, 4, is_ctx)) + _mod_row(m_ref, 3, is_ctx)
    h2_ref[0] = h2.astype(bf16)


def _outproj_call(ys, xx, modall, g, w_out, n_ctx, skip_rows):
    bsz, n, d = xx.shape
    tm = OUT_TM
    tile0 = skip_rows // tm
    n_out = n - skip_rows
    yspec = pl.BlockSpec((1, tm, GROUP_W), lambda b, i: (b, i + tile0, 0))
    ospec = pl.BlockSpec((1, tm, d), lambda b, i: (b, i, 0))
    return pl.pallas_call(
        functools.partial(_outproj_kernel, n_ctx=0 if skip_rows >= n_ctx else n_ctx, tile0=tile0),
        grid=(bsz, n_out // tm),
        in_specs=[yspec] * 4 + [pl.BlockSpec((1, tm, d), lambda b, i: (b, i + tile0, 0)),
                                pl.BlockSpec((1, 12, d), lambda b, i: (b, 0, 0)),
                                pl.BlockSpec((1, d), lambda b, i: (0, 0)),
                                pl.BlockSpec(w_out.shape, lambda b, i: (0, 0))],
        out_specs=[ospec, ospec],
        out_shape=[jax.ShapeDtypeStruct((bsz, n_out, d), f32), jax.ShapeDtypeStruct((bsz, n_out, d), bf16)],
        compiler_params=_cparams(("parallel", "parallel")),
        name="outproj",
    )(*ys, xx, modall, g.reshape(1, d), w_out.astype(bf16))


MLP_TF = 1024


def _mlp_kernel(h_ref, x_ref, m_ref, w1_ref, w2_ref, fg_ref, o_ref, acc_ref, *, n_ctx, tm, final_norm):
    kf = pl.program_id(2)

    @pl.when(kf == 0)
    def _():
        acc_ref[...] = jnp.zeros_like(acc_ref)

    z = jnp.maximum(jnp.dot(h_ref[0], w1_ref[...], preferred_element_type=f32), 0.0)
    acc_ref[...] += jnp.dot((z * z).astype(bf16), w2_ref[...], preferred_element_type=f32)

    @pl.when(kf == pl.num_programs(2) - 1)
    def _():
        is_ctx = _is_ctx_rows(pl.program_id(1), tm, n_ctx)
        x2 = x_ref[0] + _mod_row(m_ref, 5, is_ctx) * acc_ref[...]
        o_ref[0] = _rms(x2, fg_ref[...]) if final_norm else x2


def _mlp_call(h2, x1, modall, w1, w2, final_g, n_ctx, final_norm):
    bsz, n, d = x1.shape
    dff = w1.shape[1]
    tm = next(t for t in (1024, 768, 512, 256) if n % t == 0)
    return pl.pallas_call(
        functools.partial(_mlp_kernel, n_ctx=n_ctx, tm=tm, final_norm=final_norm),
        grid=(bsz, n // tm, dff // MLP_TF),
        in_specs=[pl.BlockSpec((1, tm, d), lambda b, i, k: (b, i, 0)),
                  pl.BlockSpec((1, tm, d), lambda b, i, k: (b, i, 0)),
                  pl.BlockSpec((1, 12, d), lambda b, i, k: (b, 0, 0)),
                  pl.BlockSpec((d, MLP_TF), lambda b, i, k: (0, k)),
                  pl.BlockSpec((MLP_TF, d), lambda b, i, k: (k, 0)),
                  pl.BlockSpec((1, d), lambda b, i, k: (0, 0))],
        out_specs=pl.BlockSpec((1, tm, d), lambda b, i, k: (b, i, 0)),
        out_shape=jax.ShapeDtypeStruct((bsz, n, d), f32),
        scratch_shapes=[pltpu.VMEM((tm, d), f32)],
        compiler_params=_cparams(("parallel", "parallel", "arbitrary")),
        name="mlp",
    )(h2, x1, modall, w1.astype(bf16), w2.astype(bf16), final_g.reshape(1, d))


def kernel(x, c, ctx, c_ctx, ada_w, ada_b, norm1_g, norm2_g, w_in, w_out, conv_dw, conv_db, conv_gn_g, conv_gn_b, conv_pw, pool_w, pool_scale, rw_mu_prev, rw_mu_next, rw_w0, rw_w2, rw_a0, rw_a2, rw_g2, rw_kk, rw_ka, rw_rk, rw_lnx_g, rw_lnx_b, mla_qn_g, mla_wuq, mla_kvn_g, mla_wukv, mlp_w1, mlp_w2, final_g):
    bsz, n_lat, d = x.shape
    n_ctx = ctx.shape[1]
    depth = ada_w.shape[0]
    n_all = n_ctx + n_lat
    assert n_ctx % max(RW_TILE, ATT_TQ, OUT_TM, CONV_ROWS) == 0 and n_lat % max(RW_TILE, ATT_TQ, OUT_TM) == 0
    assert n_lat % GRID_W == 0 and bsz + 1 <= 24

    xx = jnp.concatenate([ctx, x], axis=1)
    s_all = jnp.zeros((24, d), f32).at[:bsz].set(c).at[bsz].set(c_ctx)
    cos, sin = _rope_tables(n_ctx, n_lat)
    segs = ((0, n_ctx), (n_ctx, n_lat))
    w_in_pad = jnp.pad(w_in, ((0, 0), (0, 0), (0, (-P_IN) % LANES))).astype(bf16)

    out = None
    for l in range(depth):
        last = l == depth - 1
        mod = _ada_call(s_all, ada_w[l], ada_b[l])
        mod_lat = mod[:bsz].reshape(bsz, 6, d)
        mod_ctx = jnp.broadcast_to(mod[bsz].reshape(1, 6, d), (bsz, 6, d))
        modall = jnp.concatenate([mod_ctx, mod_lat], axis=1)

        p_conv, p_pool, p_q, p_rw, p_kv = _inproj_call(xx, modall, norm1_g[l], w_in_pad[l], n_ctx)
        y_conv = _conv_call(p_conv, conv_dw[l], conv_db[l], conv_gn_g[l], conv_gn_b[l], conv_pw[l], segs)
        y_pool = _pool_call(p_pool, pool_w[l], pool_scale[l], segs)
        prep = _rwprep_call(p_rw, rw_mu_prev[l], rw_mu_next[l], rw_w0[l], rw_w2[l], rw_a0[l], rw_a2[l],
                            rw_g2[l], rw_kk[l], rw_ka[l], rw_rk[l], n_ctx)
        y0, y1 = _rwscan_call(prep[:11], n_ctx)
        y_rw = _rwpost_call(y0, y1, prep[11], prep[12], rw_lnx_g[l], rw_lnx_b[l])
        y_att = _mla_call(p_q, p_kv, cos, sin, mla_qn_g[l], mla_wuq[l], mla_kvn_g[l], mla_wukv[l], n_ctx)

        skip = n_ctx if last else 0
        x1, h2 = _outproj_call((y_conv, y_rw, y_pool, y_att), xx, modall, norm2_g[l], w_out[l], n_ctx, skip)
        res = _mlp_call(h2, x1, modall, mlp_w1[l], mlp_w2[l], final_g, 0 if last else n_ctx, last)
        if last:
            out = res
        else:
            xx = res
    return out
```

```python
import functools

import numpy as np
import jax
import jax.numpy as jnp
from jax import lax
from jax.experimental import pallas as pl
from jax.experimental.pallas import tpu as pltpu

f32 = jnp.float32
bf16 = jnp.bfloat16
HIGHEST = lax.Precision.HIGHEST

GROUP_W = 256
NORM_EPS = 1e-6
GN_EPS = 1e-5
CONV_K = 31
CONV_NORM_GROUPS = 4
RW_HEAD = 64
RW_HEADS = GROUP_W // RW_HEAD
RW_DECAY_LORA = 64
RW_A_LORA = 64
RW_GATE_LORA = 128
RW_LNX_EPS = 64e-5
POOL_WINDOWS = (2, 4, 8, 16)
MLA_HEADS = 4
MLA_NOPE = 64
MLA_ROPE = 32
MLA_V = 64
MLA_KV_RANK = 128
ROPE_BASE = 10000.0
GRID_W = 64
N_CONV_IN = 2 * GROUP_W
N_RW_IN = 3 * GROUP_W + RW_GATE_LORA + 2 * RW_DECAY_LORA + 2 * RW_A_LORA
N_KV_IN = MLA_KV_RANK + MLA_ROPE
OFF_POOL = N_CONV_IN
OFF_Q = OFF_POOL + GROUP_W
OFF_RW = OFF_Q + GROUP_W
OFF_KV = OFF_RW + N_RW_IN
P_IN = OFF_KV + N_KV_IN

LANES = 128
SUBLANES = 8
VMEM_LIMIT_BYTES = 56 * 1024 * 1024

RW_CHUNK = 64
HEAD_PAD = 128


def _cparams(sem):
    return pltpu.CompilerParams(dimension_semantics=sem, vmem_limit_bytes=VMEM_LIMIT_BYTES)


def _dot(a, b):
    return jnp.dot(a.astype(bf16), b.astype(bf16), preferred_element_type=f32)


def _dot_hi(a, b):
    return jnp.dot(a, b, preferred_element_type=f32, precision=HIGHEST)


def _dot_nt(a, b):
    return lax.dot_general(a.astype(bf16), b.astype(bf16), (((1,), (1,)), ((), ())), preferred_element_type=f32)


def _dot_tn(a, b):
    return lax.dot_general(a.astype(bf16), b.astype(bf16), (((0,), (0,)), ((), ())), preferred_element_type=f32)


def _sigmoid(x):
    return jax.nn.sigmoid(x)


def _mod_row(m_ref, j, is_ctx):
    lat = m_ref[0, 6 + j:7 + j, :]
    if is_ctx is None:
        return lat
    return jnp.where(is_ctx, m_ref[0, j:j + 1, :], lat)


def _is_ctx_rows(tile_idx, tm, n_ctx):
    if n_ctx == 0:
        return None
    row = tile_idx * tm + lax.broadcasted_iota(jnp.int32, (tm, 1), 0)
    return row < n_ctx


def _rms(x, g):
    return x * lax.rsqrt(jnp.mean(x * x, axis=-1, keepdims=True) + NORM_EPS) * g


def _ada_kernel(s_ref, w_ref, b_ref, o_ref):
    s = s_ref[...]
    s = s * _sigmoid(s)
    o_ref[...] = _dot_hi(s, w_ref[...]) + b_ref[...]


def _ada_call(s_all, w, b):
    rows, d = s_all.shape
    n = w.shape[1]
    tn = 1536
    return pl.pallas_call(
        _ada_kernel,
        grid=(n // tn,),
        in_specs=[pl.BlockSpec((rows, d), lambda j: (0, 0)),
                  pl.BlockSpec((d, tn), lambda j: (0, j)),
                  pl.BlockSpec((1, tn), lambda j: (0, j))],
        out_specs=pl.BlockSpec((rows, tn), lambda j: (0, j)),
        out_shape=jax.ShapeDtypeStruct((rows, n), f32),
        compiler_params=_cparams(("arbitrary",)),
        name="ada",
    )(s_all, w, b.reshape(1, n))


def _inproj_kernel(x_ref, m_ref, g_ref, w_ref, oc_ref, op_ref, oq_ref, orw_ref, okv_ref, *, n_ctx, tm):
    is_ctx = _is_ctx_rows(pl.program_id(1), tm, n_ctx)
    h = _rms(x_ref[0], g_ref[...]) * (1.0 + _mod_row(m_ref, 1, is_ctx)) + _mod_row(m_ref, 0, is_ctx)
    p = jnp.dot(h.astype(bf16), w_ref[...], preferred_element_type=f32)
    oc_ref[0] = p[:, 0:OFF_POOL]
    op_ref[0] = p[:, OFF_POOL:OFF_Q]
    oq_ref[0] = p[:, OFF_Q:OFF_RW]
    orw_ref[0] = p[:, OFF_RW:OFF_KV]
    okv_ref[0] = p[:, OFF_KV:P_IN]


def _inproj_call(xx, modall, g, w_pad, n_ctx):
    bsz, n, d = xx.shape
    tm = 384 if n % 384 == 0 else 256
    widths = (N_CONV_IN, GROUP_W, GROUP_W, N_RW_IN, N_KV_IN)
    return pl.pallas_call(
        functools.partial(_inproj_kernel, n_ctx=n_ctx, tm=tm),
        grid=(bsz, n // tm),
        in_specs=[pl.BlockSpec((1, tm, d), lambda b, i: (b, i, 0)),
                  pl.BlockSpec((1, 12, d), lambda b, i: (b, 0, 0)),
                  pl.BlockSpec((1, d), lambda b, i: (0, 0)),
                  pl.BlockSpec(w_pad.shape, lambda b, i: (0, 0))],
        out_specs=[pl.BlockSpec((1, tm, w), lambda b, i: (b, i, 0)) for w in widths],
        out_shape=[jax.ShapeDtypeStruct((bsz, n, w), f32) for w in widths],
        compiler_params=_cparams(("parallel", "parallel")),
        name="inproj",
    )(xx, modall, g.reshape(1, d), w_pad)


CONV_ROWS = 128
CONV_HALO = 16


def _conv_kernel(p_ref, dw_ref, db_ref, gg_ref, gb_ref, pw_ref, gavg_ref, o_ref, u_scr, *, segs):
    r, hl = CONV_ROWS, CONV_HALO
    win_rows = r + 2 * hl
    for s0, n in segs:
        u_scr[0:hl, :] = jnp.zeros((hl, GROUP_W), f32)
        u_scr[hl + n:hl + n + hl, :] = jnp.zeros((hl, GROUP_W), f32)

        def fill(c, carry, s0=s0):
            r0 = pl.multiple_of(c * r, r)
            blk = p_ref[0, pl.ds(s0 + r0, r), :]
            u_scr[pl.ds(hl + r0, r), :] = blk[:, :GROUP_W] * _sigmoid(blk[:, GROUP_W:])
            return carry

        lax.fori_loop(0, n // r, fill, 0)

        def body(c, carry, s0=s0):
            r0 = pl.multiple_of(c * r, r)
            win = u_scr[pl.ds(r0, win_rows), :]
            acc = jnp.zeros((r, GROUP_W), f32)
            for j in range(CONV_K):
                off = hl - CONV_K // 2 + j
                tap = pltpu.roll(win, (win_rows - off) % win_rows, axis=0)[0:r]
                acc = acc + tap * dw_ref[j:j + 1, :]
            y = acc + db_ref[...]
            mu = _dot_hi(y, gavg_ref[...])
            dlt = y - mu
            var = _dot_hi(dlt * dlt, gavg_ref[...])
            yn = dlt * lax.rsqrt(var + GN_EPS) * gg_ref[...] + gb_ref[...]
            act = yn * _sigmoid(yn)
            o_ref[0, pl.ds(s0 + r0, r), :] = _dot(act, pw_ref[...]).astype(bf16)
            return carry

        lax.fori_loop(0, n // r, body, 0)


def _group_avg_matrix(width, group):
    idx = np.arange(width) // group
    return jnp.asarray((idx[:, None] == idx[None, :]).astype(np.float32) / group)


def _conv_call(p_conv, dw, db, gg, gb, pw, segs):
    bsz, n, _ = p_conv.shape
    gavg = _group_avg_matrix(GROUP_W, GROUP_W // CONV_NORM_GROUPS)
    max_seg = max(s[1] for s in segs)
    vec = lambda: pl.BlockSpec((1, GROUP_W), lambda b: (0, 0))
    return pl.pallas_call(
        functools.partial(_conv_kernel, segs=segs),
        grid=(bsz,),
        in_specs=[pl.BlockSpec((1, n, N_CONV_IN), lambda b: (b, 0, 0)),
                  pl.BlockSpec((CONV_K, GROUP_W), lambda b: (0, 0)),
                  vec(), vec(), vec(),
                  pl.BlockSpec((GROUP_W, GROUP_W), lambda b: (0, 0)),
                  pl.BlockSpec((GROUP_W, GROUP_W), lambda b: (0, 0))],
        out_specs=pl.BlockSpec((1, n, GROUP_W), lambda b: (b, 0, 0)),
        out_shape=jax.ShapeDtypeStruct((bsz, n, GROUP_W), bf16),
        scratch_shapes=[pltpu.VMEM((max_seg + 2 * CONV_HALO, GROUP_W), f32)],
        compiler_params=_cparams(("parallel",)),
        name="conv",
    )(p_conv, dw, db.reshape(1, -1), gg.reshape(1, -1), gb.reshape(1, -1), pw.astype(bf16), gavg)


def _pool_kernel(p_ref, w_ref, sc_ref, o_ref, u_scr, *, segs):
    r, hl = CONV_ROWS, CONV_HALO
    win_rows = r + 2 * hl
    pool_ch = GROUP_W // len(POOL_WINDOWS)
    lane = lax.broadcasted_iota(jnp.int32, (1, GROUP_W), 1)
    half = jnp.full((1, GROUP_W), POOL_WINDOWS[-1] // 2, jnp.int32)
    for gi in range(len(POOL_WINDOWS) - 2, -1, -1):
        half = jnp.where(lane < (gi + 1) * pool_ch, POOL_WINDOWS[gi] // 2, half)

    def shifted(v, k):
        return pltpu.roll(v, (win_rows - k) % win_rows, axis=0)

    for s0, n in segs:
        u_scr[0:hl, :] = jnp.zeros((hl, GROUP_W), f32)
        u_scr[hl + n:hl + n + hl, :] = jnp.zeros((hl, GROUP_W), f32)

        def fill(c, carry, s0=s0):
            r0 = pl.multiple_of(c * r, r)
            u_scr[pl.ds(hl + r0, r), :] = p_ref[0, pl.ds(s0 + r0, r), :]
            return carry

        lax.fori_loop(0, n // r, fill, 0)

        def body(c, carry, s0=s0, n=n):
            r0 = pl.multiple_of(c * r, r)
            win = u_scr[pl.ds(r0, win_rows), :]
            s2 = win + shifted(win, -1)
            s4 = shifted(s2, -1) + shifted(s2, 1)
            s8 = shifted(s4, -2) + shifted(s4, 2)
            s16 = shifted(s8, -4) + shifted(s8, 4)
            sums = (s2, s4, s8, s16)
            sel = sums[-1]
            for gi in range(len(POOL_WINDOWS) - 2, -1, -1):
                sel = jnp.where(lane < (gi + 1) * pool_ch, sums[gi], sel)
            sel = sel[hl:hl + r]
            u = win[hl:hl + r]
            t = r0 + lax.broadcasted_iota(jnp.int32, (r, 1), 0)
            cnt = (jnp.minimum(t + half, n) - jnp.maximum(t - half, 0)).astype(f32)
            dlt = sel / cnt - u
            o_ref[0, pl.ds(s0 + r0, r), :] = (_dot(dlt, w_ref[...]) * sc_ref[...]).astype(bf16)
            return carry

        lax.fori_loop(0, n // r, body, 0)


def _block_diag(blocks):
    g, a, b = blocks.shape
    out = jnp.zeros((g * a, g * b), blocks.dtype)
    for i in range(g):
        out = out.at[i * a:(i + 1) * a, i * b:(i + 1) * b].set(blocks[i])
    return out


def _pool_call(p_pool, pool_w, pool_scale, segs):
    bsz, n, _ = p_pool.shape
    max_seg = max(s[1] for s in segs)
    return pl.pallas_call(
        functools.partial(_pool_kernel, segs=segs),
        grid=(bsz,),
        in_specs=[pl.BlockSpec((1, n, GROUP_W), lambda b: (b, 0, 0)),
                  pl.BlockSpec((GROUP_W, GROUP_W), lambda b: (0, 0)),
                  pl.BlockSpec((1, GROUP_W), lambda b: (0, 0))],
        out_specs=pl.BlockSpec((1, n, GROUP_W), lambda b: (b, 0, 0)),
        out_shape=jax.ShapeDtypeStruct((bsz, n, GROUP_W), bf16),
        scratch_shapes=[pltpu.VMEM((max_seg + 2 * CONV_HALO, GROUP_W), f32)],
        compiler_params=_cparams(("parallel",)),
        name="pool",
    )(p_pool, _block_diag(pool_w).astype(bf16), pool_scale.reshape(1, -1))


RW_TILE = 256
RW_R, RW_K, RW_V, RW_G = 0, GROUP_W, 2 * GROUP_W, 3 * GROUP_W
RW_W = RW_G + RW_GATE_LORA
RW_A = RW_W + 2 * RW_DECAY_LORA
RW_STACK = RW_HEADS * RW_CHUNK


def _stack_heads(x, hm):
    return jnp.concatenate([x] * RW_HEADS, axis=0) * hm


def _unstack_heads(z):
    c = z.shape[0] // RW_HEADS
    out = z[0:c]
    for h in range(1, RW_HEADS):
        out = out + z[h * c:(h + 1) * c]
    return out


def _scan_order_masks():
    ri = lax.broadcasted_iota(jnp.int32, (RW_STACK, RW_STACK), 0)
    ci = lax.broadcasted_iota(jnp.int32, (RW_STACK, RW_STACK), 1)
    return ((ri > ci, ri >= ci), (ri < ci, ri <= ci)), (ri == ci).astype(f32)


def _head_mask():
    m = np.arange(RW_STACK)[:, None] // RW_CHUNK == np.arange(GROUP_W)[None, :] // RW_HEAD
    return jnp.asarray(m.astype(np.float32))


def _rwprep_kernel(p_ref, hp_ref, hn_ref, mup_ref, mun_ref, w0_ref, w2_ref, a0_ref, a2_ref, g2_ref,
                   kk_ref, ka_ref, rk_ref, tril_ref, triu_ref, hsum_ref, hm_ref,
                   rt0, bt0, bp0, kp0, kh0, wc0, yp0, pc0, rt1, bt1, bp1, kp1, kh1, wc1, yp1, pc1,
                   v_out, bonus_out, gate_out, *, seg_starts, seg_ends):
    tr, c = RW_TILE, RW_CHUNK
    i = pl.program_id(1)
    row0 = i * tr
    first = functools.reduce(jnp.logical_or, [row0 == s for s in seg_starts])
    last = functools.reduce(jnp.logical_or, [row0 + tr == e for e in seg_ends])
    ridx = lax.broadcasted_iota(jnp.int32, (tr, 1), 0)

    def zcols(a, b):
        p = p_ref[0, :, a:b]
        prev_row = jnp.where(first, 0.0, hp_ref[0, SUBLANES - 1:SUBLANES, a:b])
        next_row = jnp.where(last, 0.0, hn_ref[0, 0:1, a:b])
        prev = jnp.where(ridx == 0, prev_row, pltpu.roll(p, 1, axis=0))
        nxt = jnp.where(ridx == tr - 1, next_row, pltpu.roll(p, tr - 1, axis=0))
        return p + mup_ref[:, a:b] * (prev - p) + mun_ref[:, a:b] * (nxt - p)

    r = zcols(RW_R, RW_K)
    k = zcols(RW_K, RW_V)
    v = zcols(RW_V, RW_G)
    vb = v.astype(bf16)
    v_out[0] = vb
    gate_out[0] = _dot(_sigmoid(zcols(RW_G, RW_W)), g2_ref[...])
    kk = k * kk_ref[...]
    kk = kk / jnp.maximum(jnp.sqrt(_dot_hi(kk * kk, hsum_ref[...])), 1e-12)
    w_all = _dot_hi(jnp.tanh(zcols(RW_W, RW_A)), w2_ref[...]) + w0_ref[...]
    a_all = _sigmoid(_dot_hi(zcols(RW_A, N_RW_IN), a2_ref[...]) + a0_ref[...])
    kd_sum = jnp.zeros_like(k)
    outs = ((rt0, bt0, bp0, kp0, kh0, wc0, yp0, pc0, tril_ref), (rt1, bt1, bp1, kp1, kh1, wc1, yp1, pc1, triu_ref))
    scaled = []
    for d, (rt_o, bt_o, bp_o, kp_o, _, _, _, pc_o, tri_ref) in enumerate(outs):
        x = w_all[:, d * GROUP_W:(d + 1) * GROUP_W]
        neg = -x
        log_w = -(jnp.maximum(neg, 0.0) + jnp.log1p(jnp.exp(-jnp.abs(neg)))) - 0.5
        lw = -jnp.exp(log_w)
        a = a_all[:, d * GROUP_W:(d + 1) * GROUP_W]
        kd = k * (1.0 + (a - 1.0) * ka_ref[...])
        kd_sum = kd_sum + kd
        b = kk * a
        cum = _dot_hi(tri_ref[...], lw)
        tot_rows = []
        for ci in range(tr // c):
            edge = ci * c + (c - 1 if d == 0 else 0)
            tot_rows.append(cum[edge:edge + 1, :])
            pc_o[0, ci * SUBLANES:(ci + 1) * SUBLANES, :] = jnp.broadcast_to(jnp.exp(tot_rows[-1]), (SUBLANES, GROUP_W))
        tot = jnp.concatenate([jnp.broadcast_to(t, (c, GROUP_W)) for t in tot_rows], axis=0)
        e_neg = jnp.exp(-cum)
        e_rest = jnp.exp(tot - cum)
        rt = (r * jnp.exp(cum)).astype(bf16)
        kq = (kk * jnp.exp(cum - lw)).astype(bf16)
        bt = (b * e_neg).astype(bf16)
        kt = (kd * e_neg).astype(bf16)
        rt_o[0] = rt
        bt_o[0] = bt
        bp_o[0] = (b * e_rest).astype(bf16)
        kp_o[0] = (kd * e_rest).astype(bf16)
        scaled.append((rt, kq, bt, kt))
    bonus_out[0] = _dot_hi(r * rk_ref[...] * kd_sum, hsum_ref[...]) * v

    hm = hm_ref[...]
    masks, eye = _scan_order_masks()
    chains = [(d, ci) for d in range(2) for ci in range(tr // c)]
    rows = lambda ci: slice(ci * c, (ci + 1) * c)
    rs = [_stack_heads(scaled[d][0][rows(ci)], hm) for d, ci in chains]
    ks = [_stack_heads(scaled[d][1][rows(ci)], hm) for d, ci in chains]
    bs = [_stack_heads(scaled[d][2][rows(ci)], hm) for d, ci in chains]
    kts = [_stack_heads(scaled[d][3][rows(ci)], hm) for d, ci in chains]
    vs = [_stack_heads(vb[rows(ci)], hm) for d, ci in chains]
    a_ub = [jnp.where(masks[d][0], _dot_nt(ks[j], bs[j]), 0.0) for j, (d, ci) in enumerate(chains)]
    a_vk = [jnp.where(masks[d][0], _dot_nt(ks[j], kts[j]), 0.0) for j, (d, ci) in enumerate(chains)]
    a_rk = [jnp.where(masks[d][1], _dot_nt(rs[j], kts[j]), 0.0) for j, (d, ci) in enumerate(chains)]
    tinv = [eye - a for a in a_ub]
    apow = a_ub
    for _ in range(int(np.log2(c)) - 1):
        apow = [_dot(a, a) for a in apow]
        tinv = [t + _dot(t, a) for t, a in zip(tinv, apow)]
    for j, (d, ci) in enumerate(chains):
        kh_o, wc_o, yp_o = outs[d][4], outs[d][5], outs[d][6]
        kh_o[0, rows(ci), :] = _unstack_heads(_dot(tinv[j], ks[j])).astype(bf16)
        wc_o[0, rows(ci), :] = _unstack_heads(_dot(tinv[j], _dot(a_vk[j], vs[j])))
        yp_o[0, rows(ci), :] = _unstack_heads(_dot(a_rk[j], vs[j]))


def _chunk_tri(tile, chunk, upper):
    t = np.arange(tile)
    same = (t[:, None] // chunk) == (t[None, :] // chunk)
    tri = (t[None, :] >= t[:, None]) if upper else (t[None, :] <= t[:, None])
    return jnp.asarray((same & tri).astype(np.float32))


def _rwprep_call(p_rw, mu_prev, mu_next, w0, w2, a0, a2, g2, kkp, kap, rk, n_ctx):
    bsz, n, _ = p_rw.shape
    tr = RW_TILE
    nb8 = n // SUBLANES
    seg_starts = tuple(sorted({0, n_ctx}))
    seg_ends = tuple(sorted({n_ctx, n} - {0}))
    zeros = jnp.zeros((RW_DECAY_LORA, GROUP_W), f32)
    w2cat = jnp.concatenate([jnp.concatenate([w2[0], zeros], 1), jnp.concatenate([zeros, w2[1]], 1)], 0)
    a2cat = jnp.concatenate([jnp.concatenate([a2[0], zeros], 1), jnp.concatenate([zeros, a2[1]], 1)], 0)
    hsum = _group_avg_matrix(GROUP_W, RW_HEAD) * RW_HEAD
    full = lambda shape: pl.BlockSpec(shape, lambda b, i: tuple(0 for _ in shape))
    row_spec = pl.BlockSpec((1, tr, GROUP_W), lambda b, i: (b, i, 0))
    pc_rows = tr // RW_CHUNK * SUBLANES
    pc_spec = pl.BlockSpec((1, pc_rows, GROUP_W), lambda b, i: (b, i, 0))
    arr = lambda dt: jax.ShapeDtypeStruct((bsz, n, GROUP_W), dt)
    pc_arr = jax.ShapeDtypeStruct((bsz, n // RW_CHUNK * SUBLANES, GROUP_W), f32)
    dir_specs = [row_spec] * 7 + [pc_spec]
    dir_shapes = [arr(bf16)] * 5 + [arr(f32), arr(f32), pc_arr]
    return pl.pallas_call(
        functools.partial(_rwprep_kernel, seg_starts=seg_starts, seg_ends=seg_ends),
        grid=(bsz, n // tr),
        in_specs=[pl.BlockSpec((1, tr, N_RW_IN), lambda b, i: (b, i, 0)),
                  pl.BlockSpec((1, SUBLANES, N_RW_IN),
                               lambda b, i: (b, jnp.maximum(i * (tr // SUBLANES) - 1, 0), 0)),
                  pl.BlockSpec((1, SUBLANES, N_RW_IN),
                               lambda b, i: (b, jnp.minimum((i + 1) * (tr // SUBLANES), nb8 - 1), 0)),
                  full((1, N_RW_IN)), full((1, N_RW_IN)),
                  full((1, 2 * GROUP_W)), full((2 * RW_DECAY_LORA, 2 * GROUP_W)),
                  full((1, 2 * GROUP_W)), full((2 * RW_A_LORA, 2 * GROUP_W)),
                  full((RW_GATE_LORA, GROUP_W)),
                  full((1, GROUP_W)), full((1, GROUP_W)), full((1, GROUP_W)),
                  full((tr, tr)), full((tr, tr)), full((GROUP_W, GROUP_W)), full((RW_STACK, GROUP_W))],
        out_specs=dir_specs * 2 + [row_spec] * 3,
        out_shape=dir_shapes * 2 + [arr(bf16), arr(f32), arr(f32)],
        compiler_params=_cparams(("parallel", "parallel")),
        name="rwprep",
    )(p_rw, p_rw, p_rw, mu_prev.reshape(1, -1), mu_next.reshape(1, -1),
      w0.reshape(1, -1), w2cat, a0.reshape(1, -1), a2cat, g2.astype(bf16),
      kkp.reshape(1, -1), kap.reshape(1, -1), rk.reshape(1, -1),
      _chunk_tri(tr, RW_CHUNK, False), _chunk_tri(tr, RW_CHUNK, True), hsum, _head_mask().astype(bf16))


RW_SCAN_BATCH = 2


def _rwscan_kernel(*refs):
    nd = 9
    dir_refs = (refs[0:nd], refs[nd:2 * nd])
    hm_ref, y_refs, s_scr = refs[2 * nd], refs[2 * nd + 1:2 * nd + 3], refs[2 * nd + 3]

    @pl.when(pl.program_id(1) == 0)
    def _():
        s_scr[...] = jnp.zeros_like(s_scr)

    hm = hm_ref[...]
    hm32 = hm.astype(f32)
    masks, _ = _scan_order_masks()
    chains = [(d, bb) for bb in range(RW_SCAN_BATCH) for d in range(2)]
    ld = lambda d, bb, k: dir_refs[d][k][bb]
    rs = [_stack_heads(ld(d, bb, 0), hm) for d, bb in chains]
    bs = [_stack_heads(ld(d, bb, 1), hm) for d, bb in chains]
    bps = [_stack_heads(ld(d, bb, 2), hm) for d, bb in chains]
    kps = [_stack_heads(ld(d, bb, 3), hm) for d, bb in chains]
    khs = [_stack_heads(ld(d, bb, 4), hm) for d, bb in chains]
    w2t = [_stack_heads(ld(d, bb, 5), hm32).T for d, bb in chains]
    vs = [_stack_heads(ld(d, bb, 8), hm) for d, bb in chains]
    s = [s_scr[d, bb] for d, bb in chains]
    sb = [x.astype(bf16) for x in s]
    us_t = [-(_dot_nt(sb[j], khs[j]) + w2t[j]) for j in range(len(chains))]
    a_rb = [jnp.where(masks[d][1], _dot_nt(rs[j], bs[j]), 0.0) for j, (d, bb) in enumerate(chains)]
    ds = [_dot(us_t[j], bps[j]) + _dot_tn(vs[j], kps[j]) for j in range(len(chains))]
    for j, (d, bb) in enumerate(chains):
        pc_rows = jnp.concatenate([ld(d, bb, 7)] * (GROUP_W // SUBLANES), axis=0)
        s_scr[d, bb] = s[j] * pc_rows + ds[j]
    ys = [_dot_nt(rs[j], sb[j]) + _dot(a_rb[j], us_t[j].T) for j in range(len(chains))]
    for j, (d, bb) in enumerate(chains):
        y_refs[d][bb] = _unstack_heads(ys[j]) + ld(d, bb, 6)


def _rwscan_call(prep, n_ctx):
    v = prep[16]
    bsz, n, _ = v.shape
    c = RW_CHUNK
    nb = RW_SCAN_BATCH
    assert bsz % nb == 0
    n_chunks = n // c
    ctx_chunks = n_ctx // c

    def fwd(b, i):
        return (b, i, 0)

    def bwd(b, i):
        return (b, jnp.where(i < ctx_chunks, ctx_chunks - 1 - i, n_chunks - 1 - i + ctx_chunks), 0)

    blk = (nb, c, GROUP_W)
    pcb = (nb, SUBLANES, GROUP_W)
    dir_specs = lambda im: [pl.BlockSpec(blk, im)] * 7 + [pl.BlockSpec(pcb, im), pl.BlockSpec(blk, im)]
    return pl.pallas_call(
        _rwscan_kernel,
        grid=(bsz // nb, n_chunks),
        in_specs=dir_specs(fwd) + dir_specs(bwd) + [pl.BlockSpec((RW_STACK, GROUP_W), lambda b, i: (0, 0))],
        out_specs=[pl.BlockSpec(blk, fwd), pl.BlockSpec(blk, bwd)],
        out_shape=[jax.ShapeDtypeStruct((bsz, n, GROUP_W), f32)] * 2,
        scratch_shapes=[pltpu.VMEM((2, nb, GROUP_W, GROUP_W), f32)],
        compiler_params=_cparams(("parallel", "arbitrary")),
        name="rwscan",
    )(*prep[0:8], v, *prep[8:16], v, _head_mask().astype(bf16))


def _rwpost_kernel(y0_ref, y1_ref, bonus_ref, gate_ref, g_ref, b_ref, gavg_ref, o_ref):
    y = y0_ref[0] + y1_ref[0]
    mu = _dot_hi(y, gavg_ref[...])
    dlt = y - mu
    var = _dot_hi(dlt * dlt, gavg_ref[...])
    o = dlt * lax.rsqrt(var + RW_LNX_EPS) * g_ref[...] + b_ref[...]
    o_ref[0] = ((o + bonus_ref[0]) * gate_ref[0]).astype(bf16)


def _rwpost_call(y0, y1, bonus, gate, g, b):
    bsz, n, _ = y0.shape
    tr = RW_TILE
    spec = pl.BlockSpec((1, tr, GROUP_W), lambda bb, i: (bb, i, 0))
    vec = pl.BlockSpec((1, GROUP_W), lambda bb, i: (0, 0))
    return pl.pallas_call(
        _rwpost_kernel,
        grid=(bsz, n // tr),
        in_specs=[spec] * 4 + [vec, vec, pl.BlockSpec((GROUP_W, GROUP_W), lambda bb, i: (0, 0))],
        out_specs=spec,
        out_shape=jax.ShapeDtypeStruct((bsz, n, GROUP_W), bf16),
        compiler_params=_cparams(("parallel", "parallel")),
        name="rwpost",
    )(y0, y1, bonus, gate, g.reshape(1, -1), b.reshape(1, -1), _group_avg_matrix(GROUP_W, RW_HEAD))


ATT_TQ = 256
ATT_KV_ROWS = 256


def _mla_kernel(pq_ref, pkv_ref, cos_ref, sin_ref, qg_ref, wqa_ref, wqb_ref, kvg_ref, wk_ref, wv_ref,
                e1_ref, e2_ref, o_ref, k_scr, v_scr, *, n_ctx, n_all):
    qi = pl.program_id(1)
    tq = ATT_TQ
    hp = HEAD_PAD

    def tile4(t):
        return jnp.concatenate([t] * MLA_HEADS, axis=1)

    @pl.when(qi == 0)
    def _():
        def build(c, carry):
            r0 = pl.multiple_of(c * ATT_KV_ROWS, ATT_KV_ROWS)
            pkv = pkv_ref[0, pl.ds(r0, ATT_KV_ROWS), :]
            ckv = _rms(pkv[:, :MLA_KV_RANK], kvg_ref[...])
            kr = pkv[:, MLA_KV_RANK:]
            cos = tile4(cos_ref[pl.ds(r0, ATT_KV_ROWS), :])
            sin = tile4(sin_ref[pl.ds(r0, ATT_KV_ROWS), :])
            kmat = _dot(ckv, wk_ref[...]) + _dot_hi(kr, e1_ref[...]) * cos + _dot_hi(kr, e2_ref[...]) * sin
            k_scr[pl.ds(r0, ATT_KV_ROWS), :] = kmat.astype(bf16)
            v_scr[pl.ds(r0, ATT_KV_ROWS), :] = _dot(ckv, wv_ref[...]).astype(bf16)
            return carry

        lax.fori_loop(0, n_all // ATT_KV_ROWS, build, 0)

    r0 = pl.multiple_of(qi * tq, tq)
    qn = _rms(pq_ref[0], qg_ref[...]).astype(bf16)
    cos = tile4(cos_ref[pl.ds(r0, tq), :])
    sin = tile4(sin_ref[pl.ds(r0, tq), :])
    scale = float(MLA_NOPE + MLA_ROPE) ** -0.5
    q = (jnp.dot(qn, wqa_ref[...], preferred_element_type=f32) * cos
         + jnp.dot(qn, wqb_ref[...], preferred_element_type=f32) * sin) * scale
    q = q.astype(bf16)

    def attend(n_keys):
        for pair in range(MLA_HEADS // 2):
            acc = jnp.zeros((tq, hp), f32)
            for h in (2 * pair, 2 * pair + 1):
                s = _dot_nt(q[:, h * hp:(h + 1) * hp], k_scr[0:n_keys, h * hp:(h + 1) * hp])
                e = jnp.exp(s - jnp.max(s, axis=-1, keepdims=True))
                inv = 1.0 / jnp.sum(e, axis=-1, keepdims=True)
                acc = acc + _dot(e, v_scr[0:n_keys, h * hp:(h + 1) * hp]) * inv
            o_ref[0, :, pair * hp:(pair + 1) * hp] = acc.astype(bf16)

    if n_ctx > 0:
        @pl.when(qi < n_ctx // tq)
        def _():
            attend(n_ctx)

        @pl.when(qi >= n_ctx // tq)
        def _():
            attend(n_all)
    else:
        attend(n_all)


def _rope_rotation():
    quarter = MLA_ROPE // 4
    rot = np.zeros((MLA_ROPE, MLA_ROPE), np.float32)
    for axis in range(2):
        for f in range(quarter):
            first, second = axis * 2 * quarter + f, axis * 2 * quarter + quarter + f
            rot[second, first] = -1.0
            rot[first, second] = 1.0
    return rot


def _rope_tables(n_ctx, n_lat):
    rows = n_lat // GRID_W
    row = jnp.repeat(jnp.arange(rows), GRID_W).astype(f32)
    col = jnp.tile(jnp.arange(GRID_W), rows).astype(f32)
    n_freq = MLA_ROPE // 4
    freq = ROPE_BASE ** (-jnp.arange(n_freq, dtype=f32) / n_freq)
    ang = jnp.concatenate([row[:, None] * freq, row[:, None] * freq, col[:, None] * freq, col[:, None] * freq], 1)
    cos = jnp.ones((n_ctx + n_lat, HEAD_PAD), f32).at[n_ctx:, MLA_NOPE:MLA_NOPE + MLA_ROPE].set(jnp.cos(ang))
    sin = jnp.zeros((n_ctx + n_lat, HEAD_PAD), f32).at[n_ctx:, MLA_NOPE:MLA_NOPE + MLA_ROPE].set(jnp.sin(ang))
    return cos, sin


def _mla_weights(wuq, wukv):
    rot = _rope_rotation()
    src = np.argmax(np.abs(rot), axis=0)
    sign = jnp.asarray(rot[src, np.arange(MLA_ROPE)])
    hq = MLA_NOPE + MLA_ROPE
    hkv = MLA_NOPE + MLA_V
    wqa = jnp.zeros((wuq.shape[0], MLA_HEADS * HEAD_PAD), f32)
    wqb = jnp.zeros_like(wqa)
    wk = jnp.zeros((MLA_KV_RANK, MLA_HEADS * HEAD_PAD), f32)
    wv = jnp.zeros_like(wk)
    e1 = np.zeros((MLA_ROPE, MLA_HEADS * HEAD_PAD), np.float32)
    for h in range(MLA_HEADS):
        c0 = h * HEAD_PAD
        wqa = wqa.at[:, c0:c0 + hq].set(wuq[:, h * hq:(h + 1) * hq])
        wqb = wqb.at[:, c0 + MLA_NOPE:c0 + hq].set(wuq[:, h * hq + MLA_NOPE:(h + 1) * hq][:, src] * sign)
        wk = wk.at[:, c0:c0 + MLA_NOPE].set(wukv[:, h * hkv:h * hkv + MLA_NOPE])
        v0 = c0 + (h % 2) * MLA_V
        wv = wv.at[:, v0:v0 + MLA_V].set(wukv[:, h * hkv + MLA_NOPE:(h + 1) * hkv])
        e1[np.arange(MLA_ROPE), c0 + MLA_NOPE + np.arange(MLA_ROPE)] = 1.0
    return wqa.astype(bf16), wqb.astype(bf16), wk.astype(bf16), wv.astype(bf16), jnp.asarray(e1), jnp.asarray(rot @ e1)


def _mla_call(p_q, p_kv, cos, sin, qn_g, wuq, kvn_g, wukv, n_ctx):
    bsz, n, q_rank = p_q.shape
    wqa, wqb, wk, wv, e1, e2 = _mla_weights(wuq, wukv)
    hw = MLA_HEADS * HEAD_PAD
    full = lambda shape: pl.BlockSpec(shape, lambda b, i: tuple(0 for _ in shape))
    return pl.pallas_call(
        functools.partial(_mla_kernel, n_ctx=n_ctx, n_all=n),
        grid=(bsz, n // ATT_TQ),
        in_specs=[pl.BlockSpec((1, ATT_TQ, q_rank), lambda b, i: (b, i, 0)),
                  pl.BlockSpec((1, n, N_KV_IN), lambda b, i: (b, 0, 0)),
                  full((n, HEAD_PAD)), full((n, HEAD_PAD)),
                  full((1, q_rank)), full((q_rank, hw)), full((q_rank, hw)),
                  full((1, MLA_KV_RANK)), full((MLA_KV_RANK, hw)), full((MLA_KV_RANK, hw)),
                  full((MLA_ROPE, hw)), full((MLA_ROPE, hw))],
        out_specs=pl.BlockSpec((1, ATT_TQ, GROUP_W), lambda b, i: (b, i, 0)),
        out_shape=jax.ShapeDtypeStruct((bsz, n, GROUP_W), bf16),
        scratch_shapes=[pltpu.VMEM((n, hw), bf16), pltpu.VMEM((n, hw), bf16)],
        compiler_params=_cparams(("parallel", "arbitrary")),
        name="mla",
    )(p_q, p_kv, cos, sin, qn_g.reshape(1, -1), wqa, wqb, kvn_g.reshape(1, -1), wk, wv, e1, e2)


OUT_TM = 256


def _outproj_kernel(yc_ref, yr_ref, yp_ref, ya_ref, x_ref, m_ref, g_ref, w_ref, x1_ref, h2_ref, *, n_ctx, tile0):
    is_ctx = _is_ctx_rows(pl.program_id(1) + tile0, OUT_TM, n_ctx)
    y = jnp.concatenate([yc_ref[0], yr_ref[0], yp_ref[0], ya_ref[0]], axis=1)
    x1 = x_ref[0] + _mod_row(m_ref, 2, is_ctx) * jnp.dot(y, w_ref[...], preferred_element_type=f32)
    x1_ref[0] = x1
    h2 = _rms(x1, g_ref[...]) * (1.0 + _mod_row(m_ref, 4, is_ctx)) + _mod_row(m_ref, 3, is_ctx)
    h2_ref[0] = h2.astype(bf16)


def _outproj_call(ys, xx, modall, g, w_out, n_ctx, skip_rows):
    bsz, n, d = xx.shape
    tm = OUT_TM
    tile0 = skip_rows // tm
    n_out = n - skip_rows
    yspec = pl.BlockSpec((1, tm, GROUP_W), lambda b, i: (b, i + tile0, 0))
    ospec = pl.BlockSpec((1, tm, d), lambda b, i: (b, i, 0))
    return pl.pallas_call(
        functools.partial(_outproj_kernel, n_ctx=0 if skip_rows >= n_ctx else n_ctx, tile0=tile0),
        grid=(bsz, n_out // tm),
        in_specs=[yspec] * 4 + [pl.BlockSpec((1, tm, d), lambda b, i: (b, i + tile0, 0)),
                                pl.BlockSpec((1, 12, d), lambda b, i: (b, 0, 0)),
                                pl.BlockSpec((1, d), lambda b, i: (0, 0)),
                                pl.BlockSpec(w_out.shape, lambda b, i: (0, 0))],
        out_specs=[ospec, ospec],
        out_shape=[jax.ShapeDtypeStruct((bsz, n_out, d), f32), jax.ShapeDtypeStruct((bsz, n_out, d), bf16)],
        compiler_params=_cparams(("parallel", "parallel")),
        name="outproj",
    )(*ys, xx, modall, g.reshape(1, d), w_out.astype(bf16))


MLP_TF = 1024


def _mlp_kernel(h_ref, x_ref, m_ref, w1_ref, w2_ref, fg_ref, o_ref, acc_ref, *, n_ctx, tm, final_norm):
    kf = pl.program_id(2)

    @pl.when(kf == 0)
    def _():
        acc_ref[...] = jnp.zeros_like(acc_ref)

    z = jnp.maximum(jnp.dot(h_ref[0], w1_ref[...], preferred_element_type=f32), 0.0)
    acc_ref[...] += jnp.dot((z * z).astype(bf16), w2_ref[...], preferred_element_type=f32)

    @pl.when(kf == pl.num_programs(2) - 1)
    def _():
        is_ctx = _is_ctx_rows(pl.program_id(1), tm, n_ctx)
        x2 = x_ref[0] + _mod_row(m_ref, 5, is_ctx) * acc_ref[...]
        o_ref[0] = _rms(x2, fg_ref[...]) if final_norm else x2


def _mlp_call(h2, x1, modall, w1, w2, final_g, n_ctx, final_norm):
    bsz, n, d = x1.shape
    dff = w1.shape[1]
    tm = next(t for t in (1024, 768, 512, 256) if n % t == 0)
    return pl.pallas_call(
        functools.partial(_mlp_kernel, n_ctx=n_ctx, tm=tm, final_norm=final_norm),
        grid=(bsz, n // tm, dff // MLP_TF),
        in_specs=[pl.BlockSpec((1, tm, d), lambda b, i, k: (b, i, 0)),
                  pl.BlockSpec((1, tm, d), lambda b, i, k: (b, i, 0)),
                  pl.BlockSpec((1, 12, d), lambda b, i, k: (b, 0, 0)),
                  pl.BlockSpec((d, MLP_TF), lambda b, i, k: (0, k)),
                  pl.BlockSpec((MLP_TF, d), lambda b, i, k: (k, 0)),
                  pl.BlockSpec((1, d), lambda b, i, k: (0, 0))],
        out_specs=pl.BlockSpec((1, tm, d), lambda b, i, k: (b, i, 0)),
        out_shape=jax.ShapeDtypeStruct((bsz, n, d), f32),
        scratch_shapes=[pltpu.VMEM((tm, d), f32)],
        compiler_params=_cparams(("parallel", "parallel", "arbitrary")),
        name="mlp",
    )(h2, x1, modall, w1.astype(bf16), w2.astype(bf16), final_g.reshape(1, d))


def kernel(x, c, ctx, c_ctx, ada_w, ada_b, norm1_g, norm2_g, w_in, w_out, conv_dw, conv_db, conv_gn_g, conv_gn_b, conv_pw, pool_w, pool_scale, rw_mu_prev, rw_mu_next, rw_w0, rw_w2, rw_a0, rw_a2, rw_g2, rw_kk, rw_ka, rw_rk, rw_lnx_g, rw_lnx_b, mla_qn_g, mla_wuq, mla_kvn_g, mla_wukv, mlp_w1, mlp_w2, final_g):
    bsz, n_lat, d = x.shape
    n_ctx = ctx.shape[1]
    depth = ada_w.shape[0]
    n_all = n_ctx + n_lat
    assert n_ctx % max(RW_TILE, ATT_TQ, OUT_TM, CONV_ROWS) == 0 and n_lat % max(RW_TILE, ATT_TQ, OUT_TM) == 0
    assert n_lat % GRID_W == 0 and bsz + 1 <= 24

    xx = jnp.concatenate([ctx, x], axis=1)
    s_all = jnp.zeros((24, d), f32).at[:bsz].set(c).at[bsz].set(c_ctx)
    cos, sin = _rope_tables(n_ctx, n_lat)
    segs = ((0, n_ctx), (n_ctx, n_lat))
    w_in_pad = jnp.pad(w_in, ((0, 0), (0, 0), (0, (-P_IN) % LANES))).astype(bf16)

    out = None
    for l in range(depth):
        last = l == depth - 1
        mod = _ada_call(s_all, ada_w[l], ada_b[l])
        mod_lat = mod[:bsz].reshape(bsz, 6, d)
        mod_ctx = jnp.broadcast_to(mod[bsz].reshape(1, 6, d), (bsz, 6, d))
        modall = jnp.concatenate([mod_ctx, mod_lat], axis=1)

        p_conv, p_pool, p_q, p_rw, p_kv = _inproj_call(xx, modall, norm1_g[l], w_in_pad[l], n_ctx)
        y_conv = _conv_call(p_conv, conv_dw[l], conv_db[l], conv_gn_g[l], conv_gn_b[l], conv_pw[l], segs)
        y_pool = _pool_call(p_pool, pool_w[l], pool_scale[l], segs)
        prep = _rwprep_call(p_rw, rw_mu_prev[l], rw_mu_next[l], rw_w0[l], rw_w2[l], rw_a0[l], rw_a2[l],
                            rw_g2[l], rw_kk[l], rw_ka[l], rw_rk[l], n_ctx)
        y0, y1 = _rwscan_call(prep, n_ctx)
        y_rw = _rwpost_call(y0, y1, prep[17], prep[18], rw_lnx_g[l], rw_lnx_b[l])
        y_att = _mla_call(p_q, p_kv, cos, sin, mla_qn_g[l], mla_wuq[l], mla_kvn_g[l], mla_wukv[l], n_ctx)

        skip = n_ctx if last else 0
        x1, h2 = _outproj_call((y_conv, y_rw, y_pool, y_att), xx, modall, norm2_g[l], w_out[l], n_ctx, skip)
        res = _mlp_call(h2, x1, modall, mlp_w1[l], mlp_w2[l], final_g, 0 if last else n_ctx, last)
        if last:
            out = res
        else:
            xx = res
    return out
```

```python
import functools

import numpy as np
import jax
import jax.numpy as jnp
from jax import lax
from jax.experimental import pallas as pl
from jax.experimental.pallas import tpu as pltpu

f32 = jnp.float32
bf16 = jnp.bfloat16
HIGHEST = lax.Precision.HIGHEST

GROUP_W = 256
NORM_EPS = 1e-6
GN_EPS = 1e-5
CONV_K = 31
CONV_NORM_GROUPS = 4
RW_HEAD = 64
RW_HEADS = GROUP_W // RW_HEAD
RW_DECAY_LORA = 64
RW_A_LORA = 64
RW_GATE_LORA = 128
RW_LNX_EPS = 64e-5
POOL_WINDOWS = (2, 4, 8, 16)
MLA_HEADS = 4
MLA_NOPE = 64
MLA_ROPE = 32
MLA_V = 64
MLA_KV_RANK = 128
ROPE_BASE = 10000.0
GRID_W = 64
N_CONV_IN = 2 * GROUP_W
N_RW_IN = 3 * GROUP_W + RW_GATE_LORA + 2 * RW_DECAY_LORA + 2 * RW_A_LORA
N_KV_IN = MLA_KV_RANK + MLA_ROPE
OFF_POOL = N_CONV_IN
OFF_Q = OFF_POOL + GROUP_W
OFF_RW = OFF_Q + GROUP_W
OFF_KV = OFF_RW + N_RW_IN
P_IN = OFF_KV + N_KV_IN

LANES = 128
SUBLANES = 8
VMEM_LIMIT_BYTES = 56 * 1024 * 1024

RW_CHUNK = 64
HEAD_PAD = 128


def _cparams(sem):
    return pltpu.CompilerParams(dimension_semantics=sem, vmem_limit_bytes=VMEM_LIMIT_BYTES)


def _dot(a, b):
    return jnp.dot(a.astype(bf16), b.astype(bf16), preferred_element_type=f32)


def _dot_hi(a, b):
    return jnp.dot(a, b, preferred_element_type=f32, precision=HIGHEST)


def _dot_nt(a, b):
    return lax.dot_general(a.astype(bf16), b.astype(bf16), (((1,), (1,)), ((), ())), preferred_element_type=f32)


def _dot_tn(a, b):
    return lax.dot_general(a.astype(bf16), b.astype(bf16), (((0,), (0,)), ((), ())), preferred_element_type=f32)


def _dot_split(a, b, terms):
    split_lhs = a.dtype == f32
    x = a if split_lhs else b
    acc = None
    for _ in range(terms):
        piece = x.astype(bf16)
        part = jnp.dot(piece if split_lhs else a, b if split_lhs else piece, preferred_element_type=f32)
        acc = part if acc is None else acc + part
        x = x - piece.astype(f32)
    return acc


def _dot_3pass(a, b):
    a_hi, b_hi = a.astype(bf16), b.astype(bf16)
    a_lo, b_lo = (a - a_hi.astype(f32)).astype(bf16), (b - b_hi.astype(f32)).astype(bf16)
    return (jnp.dot(a_hi, b_hi, preferred_element_type=f32) + jnp.dot(a_lo, b_hi, preferred_element_type=f32)
            + jnp.dot(a_hi, b_lo, preferred_element_type=f32))


def _sigmoid(x):
    return jax.nn.sigmoid(x)


def _mod_row(m_ref, j, is_ctx):
    lat = m_ref[0, 6 + j:7 + j, :]
    if is_ctx is None:
        return lat
    return jnp.where(is_ctx, m_ref[0, j:j + 1, :], lat)


def _is_ctx_rows(tile_idx, tm, n_lat, n_rows):
    if n_rows <= n_lat:
        return None
    row = tile_idx * tm + lax.broadcasted_iota(jnp.int32, (tm, 1), 0)
    return row >= n_lat


def _group_stats_norm(y, gavg, eps):
    dlt = y - _dot_split(y, gavg, 2)
    return dlt * lax.rsqrt(_dot_split(dlt * dlt, gavg, 2) + eps)


def _rms(x, g):
    return x * lax.rsqrt(jnp.mean(x * x, axis=-1, keepdims=True) + NORM_EPS) * g


def _ada_kernel(s_ref, w_ref, b_ref, o_ref):
    s = s_ref[...]
    s = s * _sigmoid(s)
    o_ref[...] = _dot_hi(s, w_ref[...]) + b_ref[...]


def _ada_call(s_all, w, b):
    rows, d = s_all.shape
    n = w.shape[1]
    tn = 1536
    return pl.pallas_call(
        _ada_kernel,
        grid=(n // tn,),
        in_specs=[pl.BlockSpec((rows, d), lambda j: (0, 0)),
                  pl.BlockSpec((d, tn), lambda j: (0, j)),
                  pl.BlockSpec((1, tn), lambda j: (0, j))],
        out_specs=pl.BlockSpec((rows, tn), lambda j: (0, j)),
        out_shape=jax.ShapeDtypeStruct((rows, n), f32),
        compiler_params=_cparams(("arbitrary",)),
        name="ada",
    )(s_all, w, b.reshape(1, n))


def _inproj_kernel(x_ref, m_ref, g_ref, w_ref, oc_ref, op_ref, oq_ref, orw_ref, okv_ref, *, n_lat, n_rows, tm):
    is_ctx = _is_ctx_rows(pl.program_id(1), tm, n_lat, n_rows)
    h = _rms(x_ref[0], g_ref[...]) * (1.0 + _mod_row(m_ref, 1, is_ctx)) + _mod_row(m_ref, 0, is_ctx)
    p = jnp.dot(h.astype(bf16), w_ref[...], preferred_element_type=f32)
    oc_ref[0] = p[:, 0:OFF_POOL]
    op_ref[0] = p[:, OFF_POOL:OFF_Q]
    oq_ref[0] = p[:, OFF_Q:OFF_RW]
    orw_ref[0] = p[:, OFF_RW:OFF_KV]
    okv_ref[0] = p[:, OFF_KV:P_IN]


def _inproj_call(xx, modall, g, w_pad, n_lat):
    bsz, n, d = xx.shape
    tm = 384 if n % 384 == 0 else 256
    widths = (N_CONV_IN, GROUP_W, GROUP_W, N_RW_IN, N_KV_IN)
    return pl.pallas_call(
        functools.partial(_inproj_kernel, n_lat=n_lat, n_rows=n, tm=tm),
        grid=(bsz, n // tm),
        in_specs=[pl.BlockSpec((1, tm, d), lambda b, i: (b, i, 0)),
                  pl.BlockSpec((1, 12, d), lambda b, i: (b, 0, 0)),
                  pl.BlockSpec((1, d), lambda b, i: (0, 0)),
                  pl.BlockSpec(w_pad.shape, lambda b, i: (0, 0))],
        out_specs=[pl.BlockSpec((1, tm, w), lambda b, i: (b, i, 0)) for w in widths],
        out_shape=[jax.ShapeDtypeStruct((bsz, n, w), f32) for w in widths],
        compiler_params=_cparams(("parallel", "parallel")),
        name="inproj",
    )(xx, modall, g.reshape(1, d), w_pad)


CONV_ROWS = 128
CONV_HALO = 16


def _conv_kernel(p_ref, dw_ref, db_ref, gg_ref, gb_ref, pw_ref, gavg_ref, o_ref, u_scr, *, segs):
    r, hl = CONV_ROWS, CONV_HALO
    win_rows = r + 2 * hl
    for s0, n in segs:
        u_scr[0:hl, :] = jnp.zeros((hl, GROUP_W), f32)
        u_scr[hl + n:hl + n + hl, :] = jnp.zeros((hl, GROUP_W), f32)

        def fill(c, carry, s0=s0):
            r0 = pl.multiple_of(c * r, r)
            blk = p_ref[0, pl.ds(s0 + r0, r), :]
            u_scr[pl.ds(hl + r0, r), :] = blk[:, :GROUP_W] * _sigmoid(blk[:, GROUP_W:])
            return carry

        lax.fori_loop(0, n // r, fill, 0)

        def body(c, carry, s0=s0):
            r0 = pl.multiple_of(c * r, r)
            win = u_scr[pl.ds(r0, win_rows), :]
            rolled = [win] + [pltpu.roll(win, win_rows - b, axis=0) for b in range(1, SUBLANES)]
            acc = jnp.zeros((r, GROUP_W), f32)
            for j in range(CONV_K):
                off = hl - CONV_K // 2 + j
                base = off - off % SUBLANES
                acc = acc + rolled[off % SUBLANES][base:base + r] * dw_ref[j:j + 1, :]
            yn = _group_stats_norm(acc + db_ref[...], gavg_ref[...], GN_EPS) * gg_ref[...] + gb_ref[...]
            act = yn * _sigmoid(yn)
            o_ref[0, pl.ds(s0 + r0, r), :] = _dot(act, pw_ref[...]).astype(bf16)
            return carry

        lax.fori_loop(0, n // r, body, 0)


def _group_avg_matrix(width, group):
    idx = np.arange(width) // group
    return jnp.asarray((idx[:, None] == idx[None, :]).astype(np.float32) / group)


def _conv_call(p_conv, dw, db, gg, gb, pw, segs):
    bsz, n, _ = p_conv.shape
    n_out = max(s[0] + s[1] for s in segs)
    gavg = _group_avg_matrix(GROUP_W, GROUP_W // CONV_NORM_GROUPS).astype(bf16)
    max_seg = max(s[1] for s in segs)
    vec = lambda: pl.BlockSpec((1, GROUP_W), lambda b: (0, 0))
    return pl.pallas_call(
        functools.partial(_conv_kernel, segs=segs),
        grid=(bsz,),
        in_specs=[pl.BlockSpec((1, n, N_CONV_IN), lambda b: (b, 0, 0)),
                  pl.BlockSpec((CONV_K, GROUP_W), lambda b: (0, 0)),
                  vec(), vec(), vec(),
                  pl.BlockSpec((GROUP_W, GROUP_W), lambda b: (0, 0)),
                  pl.BlockSpec((GROUP_W, GROUP_W), lambda b: (0, 0))],
        out_specs=pl.BlockSpec((1, n_out, GROUP_W), lambda b: (b, 0, 0)),
        out_shape=jax.ShapeDtypeStruct((bsz, n_out, GROUP_W), bf16),
        scratch_shapes=[pltpu.VMEM((max_seg + 2 * CONV_HALO, GROUP_W), f32)],
        compiler_params=_cparams(("parallel",)),
        name="conv",
    )(p_conv, dw, db.reshape(1, -1), gg.reshape(1, -1), gb.reshape(1, -1), pw.astype(bf16), gavg)


def _pool_kernel(p_ref, w_ref, sc_ref, o_ref, u_scr, *, segs):
    r, hl = CONV_ROWS, CONV_HALO
    win_rows = r + 2 * hl
    pool_ch = GROUP_W // len(POOL_WINDOWS)
    lane = lax.broadcasted_iota(jnp.int32, (1, GROUP_W), 1)
    half = jnp.full((1, GROUP_W), POOL_WINDOWS[-1] // 2, jnp.int32)
    for gi in range(len(POOL_WINDOWS) - 2, -1, -1):
        half = jnp.where(lane < (gi + 1) * pool_ch, POOL_WINDOWS[gi] // 2, half)

    def shifted(v, k):
        return pltpu.roll(v, (win_rows - k) % win_rows, axis=0)

    for s0, n in segs:
        u_scr[0:hl, :] = jnp.zeros((hl, GROUP_W), f32)
        u_scr[hl + n:hl + n + hl, :] = jnp.zeros((hl, GROUP_W), f32)

        def fill(c, carry, s0=s0):
            r0 = pl.multiple_of(c * r, r)
            u_scr[pl.ds(hl + r0, r), :] = p_ref[0, pl.ds(s0 + r0, r), :]
            return carry

        lax.fori_loop(0, n // r, fill, 0)

        def body(c, carry, s0=s0, n=n):
            r0 = pl.multiple_of(c * r, r)
            win = u_scr[pl.ds(r0, win_rows), :]
            s2 = win + shifted(win, -1)
            s4 = shifted(s2, -1) + shifted(s2, 1)
            s8 = shifted(s4, -2) + shifted(s4, 2)
            s16 = shifted(s8, -4) + shifted(s8, 4)
            sums = (s2, s4, s8, s16)
            sel = sums[-1]
            for gi in range(len(POOL_WINDOWS) - 2, -1, -1):
                sel = jnp.where(lane < (gi + 1) * pool_ch, sums[gi], sel)
            sel = sel[hl:hl + r]
            u = win[hl:hl + r]
            t = r0 + lax.broadcasted_iota(jnp.int32, (r, 1), 0)
            cnt = (jnp.minimum(t + half, n) - jnp.maximum(t - half, 0)).astype(f32)
            dlt = sel / cnt - u
            o_ref[0, pl.ds(s0 + r0, r), :] = (_dot(dlt, w_ref[...]) * sc_ref[...]).astype(bf16)
            return carry

        lax.fori_loop(0, n // r, body, 0)


def _block_diag(blocks):
    g, a, b = blocks.shape
    out = jnp.zeros((g * a, g * b), blocks.dtype)
    for i in range(g):
        out = out.at[i * a:(i + 1) * a, i * b:(i + 1) * b].set(blocks[i])
    return out


def _pool_call(p_pool, pool_w, pool_scale, segs):
    bsz, n, _ = p_pool.shape
    n_out = max(s[0] + s[1] for s in segs)
    max_seg = max(s[1] for s in segs)
    return pl.pallas_call(
        functools.partial(_pool_kernel, segs=segs),
        grid=(bsz,),
        in_specs=[pl.BlockSpec((1, n, GROUP_W), lambda b: (b, 0, 0)),
                  pl.BlockSpec((GROUP_W, GROUP_W), lambda b: (0, 0)),
                  pl.BlockSpec((1, GROUP_W), lambda b: (0, 0))],
        out_specs=pl.BlockSpec((1, n_out, GROUP_W), lambda b: (b, 0, 0)),
        out_shape=jax.ShapeDtypeStruct((bsz, n_out, GROUP_W), bf16),
        scratch_shapes=[pltpu.VMEM((max_seg + 2 * CONV_HALO, GROUP_W), f32)],
        compiler_params=_cparams(("parallel",)),
        name="pool",
    )(p_pool, _block_diag(pool_w).astype(bf16), pool_scale.reshape(1, -1))


RW_TILE = 256
RW_R, RW_K, RW_V, RW_G = 0, GROUP_W, 2 * GROUP_W, 3 * GROUP_W
RW_W = RW_G + RW_GATE_LORA
RW_A = RW_W + 2 * RW_DECAY_LORA
RW_STACK = RW_HEADS * RW_CHUNK


def _stack_heads(x, hm):
    return jnp.concatenate([x] * RW_HEADS, axis=0) * hm


def _unstack_heads(z):
    c = z.shape[0] // RW_HEADS
    out = z[0:c]
    for h in range(1, RW_HEADS):
        out = out + z[h * c:(h + 1) * c]
    return out


def _scan_order_masks():
    ri = lax.broadcasted_iota(jnp.int32, (RW_STACK, RW_STACK), 0)
    ci = lax.broadcasted_iota(jnp.int32, (RW_STACK, RW_STACK), 1)
    return ((ri > ci, ri >= ci), (ri < ci, ri <= ci)), (ri == ci).astype(f32)


def _head_mask():
    m = np.arange(RW_STACK)[:, None] // RW_CHUNK == np.arange(GROUP_W)[None, :] // RW_HEAD
    return jnp.asarray(m.astype(np.float32))


def _rwprep_kernel(p_ref, hp_ref, hn_ref, mup_ref, mun_ref, w0_ref, w2_ref, a0_ref, a2_ref, g2_ref,
                   kk_ref, ka_ref, rk_ref, tril_ref, triu_ref, hsum_ref, hm_ref,
                   rt0, bt0, bp0, kp0, kh0, wc0, yp0, pc0, rt1, bt1, bp1, kp1, kh1, wc1, yp1, pc1,
                   v_out, bonus_out, gate_out, *, seg_starts, seg_ends):
    tr, c = RW_TILE, RW_CHUNK
    i = pl.program_id(1)
    row0 = i * tr
    first = functools.reduce(jnp.logical_or, [row0 == s for s in seg_starts])
    last = functools.reduce(jnp.logical_or, [row0 + tr == e for e in seg_ends])
    ridx = lax.broadcasted_iota(jnp.int32, (tr, 1), 0)

    def zcols(a, b):
        p = p_ref[0, :, a:b]
        prev_row = jnp.where(first, 0.0, hp_ref[0, SUBLANES - 1:SUBLANES, a:b])
        next_row = jnp.where(last, 0.0, hn_ref[0, 0:1, a:b])
        prev = jnp.where(ridx == 0, prev_row, pltpu.roll(p, 1, axis=0))
        nxt = jnp.where(ridx == tr - 1, next_row, pltpu.roll(p, tr - 1, axis=0))
        return p + mup_ref[:, a:b] * (prev - p) + mun_ref[:, a:b] * (nxt - p)

    r = zcols(RW_R, RW_K)
    k = zcols(RW_K, RW_V)
    v = zcols(RW_V, RW_G)
    vb = v.astype(bf16)
    v_out[0] = vb
    gate_out[0] = _dot(_sigmoid(zcols(RW_G, RW_W)), g2_ref[...])
    kk = k * kk_ref[...]
    kk = kk / jnp.maximum(jnp.sqrt(_dot_split(kk * kk, hsum_ref[...], 2)), 1e-12)
    w_all = _dot_3pass(jnp.tanh(zcols(RW_W, RW_A)), w2_ref[...]) + w0_ref[...]
    a_all = _sigmoid(_dot_3pass(zcols(RW_A, N_RW_IN), a2_ref[...]) + a0_ref[...])
    kd_sum = jnp.zeros_like(k)
    outs = ((rt0, bt0, bp0, kp0, kh0, wc0, yp0, pc0, tril_ref), (rt1, bt1, bp1, kp1, kh1, wc1, yp1, pc1, triu_ref))
    scaled = []
    for d, (rt_o, bt_o, bp_o, kp_o, _, _, _, pc_o, tri_ref) in enumerate(outs):
        x = w_all[:, d * GROUP_W:(d + 1) * GROUP_W]
        neg = -x
        log_w = -(jnp.maximum(neg, 0.0) + jnp.log1p(jnp.exp(-jnp.abs(neg)))) - 0.5
        lw = -jnp.exp(log_w)
        a = a_all[:, d * GROUP_W:(d + 1) * GROUP_W]
        kd = k * (1.0 + (a - 1.0) * ka_ref[...])
        kd_sum = kd_sum + kd
        b = kk * a
        cum = _dot_split(tri_ref[...], lw, 3)
        tot_rows = []
        for ci in range(tr // c):
            edge = ci * c + (c - 1 if d == 0 else 0)
            tot_rows.append(cum[edge:edge + 1, :])
            pc_o[0, ci * SUBLANES:(ci + 1) * SUBLANES, :] = jnp.broadcast_to(jnp.exp(tot_rows[-1]), (SUBLANES, GROUP_W))
        tot = jnp.concatenate([jnp.broadcast_to(t, (c, GROUP_W)) for t in tot_rows], axis=0)
        e_neg = jnp.exp(-cum)
        e_rest = jnp.exp(tot - cum)
        rt = (r * jnp.exp(cum)).astype(bf16)
        kq = (kk * jnp.exp(cum - lw)).astype(bf16)
        bt = (b * e_neg).astype(bf16)
        kt = (kd * e_neg).astype(bf16)
        rt_o[0] = rt
        bt_o[0] = bt
        bp_o[0] = (b * e_rest).astype(bf16)
        kp_o[0] = (kd * e_rest).astype(bf16)
        scaled.append((rt, kq, bt, kt))
    bonus_out[0] = _dot_split(r * rk_ref[...] * kd_sum, hsum_ref[...], 2) * v

    hm = hm_ref[...]
    masks, eye = _scan_order_masks()
    chains = [(d, ci) for d in range(2) for ci in range(tr // c)]
    rows = lambda ci: slice(ci * c, (ci + 1) * c)
    rs = [_stack_heads(scaled[d][0][rows(ci)], hm) for d, ci in chains]
    ks = [_stack_heads(scaled[d][1][rows(ci)], hm) for d, ci in chains]
    bs = [_stack_heads(scaled[d][2][rows(ci)], hm) for d, ci in chains]
    kts = [_stack_heads(scaled[d][3][rows(ci)], hm) for d, ci in chains]
    vs = [_stack_heads(vb[rows(ci)], hm) for d, ci in chains]
    a_ub = [jnp.where(masks[d][0], _dot_nt(ks[j], bs[j]), 0.0) for j, (d, ci) in enumerate(chains)]
    a_vk = [jnp.where(masks[d][0], _dot_nt(ks[j], kts[j]), 0.0) for j, (d, ci) in enumerate(chains)]
    a_rk = [jnp.where(masks[d][1], _dot_nt(rs[j], kts[j]), 0.0) for j, (d, ci) in enumerate(chains)]
    tinv = [eye - a for a in a_ub]
    apow = a_ub
    for _ in range(int(np.log2(c)) - 1):
        apow = [_dot(a, a) for a in apow]
        tinv = [t + _dot(t, a) for t, a in zip(tinv, apow)]
    for j, (d, ci) in enumerate(chains):
        kh_o, wc_o, yp_o = outs[d][4], outs[d][5], outs[d][6]
        kh_o[0, rows(ci), :] = _unstack_heads(_dot(tinv[j], ks[j])).astype(bf16)
        wc_o[0, rows(ci), :] = _unstack_heads(_dot(tinv[j], _dot(a_vk[j], vs[j])))
        yp_o[0, rows(ci), :] = _unstack_heads(_dot(a_rk[j], vs[j]))


def _chunk_tri(tile, chunk, upper):
    t = np.arange(tile)
    same = (t[:, None] // chunk) == (t[None, :] // chunk)
    tri = (t[None, :] >= t[:, None]) if upper else (t[None, :] <= t[:, None])
    return jnp.asarray((same & tri).astype(np.float32))


def _rwprep_call(p_rw, mu_prev, mu_next, w0, w2, a0, a2, g2, kkp, kap, rk, n_lat):
    bsz, n, _ = p_rw.shape
    tr = RW_TILE
    nb8 = n // SUBLANES
    seg_starts = tuple(sorted({0, n_lat} - {n}))
    seg_ends = tuple(sorted({n_lat, n}))
    zeros = jnp.zeros((RW_DECAY_LORA, GROUP_W), f32)
    w2cat = jnp.concatenate([jnp.concatenate([w2[0], zeros], 1), jnp.concatenate([zeros, w2[1]], 1)], 0)
    a2cat = jnp.concatenate([jnp.concatenate([a2[0], zeros], 1), jnp.concatenate([zeros, a2[1]], 1)], 0)
    hsum = _group_avg_matrix(GROUP_W, RW_HEAD) * RW_HEAD
    full = lambda shape: pl.BlockSpec(shape, lambda b, i: tuple(0 for _ in shape))
    row_spec = pl.BlockSpec((1, tr, GROUP_W), lambda b, i: (b, i, 0))
    pc_rows = tr // RW_CHUNK * SUBLANES
    pc_spec = pl.BlockSpec((1, pc_rows, GROUP_W), lambda b, i: (b, i, 0))
    arr = lambda dt: jax.ShapeDtypeStruct((bsz, n, GROUP_W), dt)
    pc_arr = jax.ShapeDtypeStruct((bsz, n // RW_CHUNK * SUBLANES, GROUP_W), f32)
    dir_specs = [row_spec] * 7 + [pc_spec]
    dir_shapes = [arr(bf16)] * 5 + [arr(f32), arr(f32), pc_arr]
    return pl.pallas_call(
        functools.partial(_rwprep_kernel, seg_starts=seg_starts, seg_ends=seg_ends),
        grid=(bsz, n // tr),
        in_specs=[pl.BlockSpec((1, tr, N_RW_IN), lambda b, i: (b, i, 0)),
                  pl.BlockSpec((1, SUBLANES, N_RW_IN),
                               lambda b, i: (b, jnp.maximum(i * (tr // SUBLANES) - 1, 0), 0)),
                  pl.BlockSpec((1, SUBLANES, N_RW_IN),
                               lambda b, i: (b, jnp.minimum((i + 1) * (tr // SUBLANES), nb8 - 1), 0)),
                  full((1, N_RW_IN)), full((1, N_RW_IN)),
                  full((1, 2 * GROUP_W)), full((2 * RW_DECAY_LORA, 2 * GROUP_W)),
                  full((1, 2 * GROUP_W)), full((2 * RW_A_LORA, 2 * GROUP_W)),
                  full((RW_GATE_LORA, GROUP_W)),
                  full((1, GROUP_W)), full((1, GROUP_W)), full((1, GROUP_W)),
                  full((tr, tr)), full((tr, tr)), full((GROUP_W, GROUP_W)), full((RW_STACK, GROUP_W))],
        out_specs=dir_specs * 2 + [row_spec] * 3,
        out_shape=dir_shapes * 2 + [arr(bf16), arr(f32), arr(f32)],
        compiler_params=_cparams(("parallel", "parallel")),
        name="rwprep",
    )(p_rw, p_rw, p_rw, mu_prev.reshape(1, -1), mu_next.reshape(1, -1),
      w0.reshape(1, -1), w2cat, a0.reshape(1, -1), a2cat, g2.astype(bf16),
      kkp.reshape(1, -1), kap.reshape(1, -1), rk.reshape(1, -1),
      _chunk_tri(tr, RW_CHUNK, False).astype(bf16), _chunk_tri(tr, RW_CHUNK, True).astype(bf16),
      hsum.astype(bf16), _head_mask().astype(bf16))


RW_SCAN_BATCH = 8


def _rwscan_kernel(*refs, nb):
    nd = 9
    dir_refs = (refs[0:nd], refs[nd:2 * nd])
    hm_ref, y_refs, s_scr = refs[2 * nd], refs[2 * nd + 1:2 * nd + 3], refs[2 * nd + 3]

    @pl.when(pl.program_id(1) == 0)
    def _():
        s_scr[...] = jnp.zeros_like(s_scr)

    hm = hm_ref[...]
    hm32 = hm.astype(f32)
    masks, _ = _scan_order_masks()
    chains = [(d, bb) for bb in range(nb) for d in range(2)]
    ld = lambda d, bb, k: dir_refs[d][k][bb]
    rs = [_stack_heads(ld(d, bb, 0), hm) for d, bb in chains]
    bs = [_stack_heads(ld(d, bb, 1), hm) for d, bb in chains]
    bps = [_stack_heads(ld(d, bb, 2), hm) for d, bb in chains]
    kps = [_stack_heads(ld(d, bb, 3), hm) for d, bb in chains]
    khs = [_stack_heads(ld(d, bb, 4), hm) for d, bb in chains]
    w2t = [_stack_heads(ld(d, bb, 5), hm32).T for d, bb in chains]
    vs = [_stack_heads(ld(d, bb, 8), hm) for d, bb in chains]
    s = [s_scr[d, bb] for d, bb in chains]
    sb = [x.astype(bf16) for x in s]
    us_t = [-(_dot_nt(sb[j], khs[j]) + w2t[j]) for j in range(len(chains))]
    a_rb = [jnp.where(masks[d][1], _dot_nt(rs[j], bs[j]), 0.0) for j, (d, bb) in enumerate(chains)]
    ds = [_dot(us_t[j], bps[j]) + _dot_tn(vs[j], kps[j]) for j in range(len(chains))]
    for j, (d, bb) in enumerate(chains):
        pc_rows = jnp.concatenate([ld(d, bb, 7)] * (GROUP_W // SUBLANES), axis=0)
        s_scr[d, bb] = s[j] * pc_rows + ds[j]
    ys = [_dot_nt(rs[j], sb[j]) + _dot(a_rb[j], us_t[j].T) for j in range(len(chains))]
    for j, (d, bb) in enumerate(chains):
        y_refs[d][bb] = _unstack_heads(ys[j]) + ld(d, bb, 6)


def _rwscan_call(prep, n_ctx):
    v = prep[16]
    bsz, n, _ = v.shape
    c = RW_CHUNK
    nb = max(t for t in range(1, RW_SCAN_BATCH + 1) if bsz % t == 0)
    n_chunks = n // c
    ctx_chunks = n_ctx // c
    lat_chunks = n_chunks - ctx_chunks

    def fwd(b, i):
        return (b, jnp.where(i < ctx_chunks, lat_chunks + i, i - ctx_chunks), 0)

    def bwd(b, i):
        return (b, n_chunks - 1 - i, 0)

    blk = (nb, c, GROUP_W)
    pcb = (nb, SUBLANES, GROUP_W)
    dir_specs = lambda im: [pl.BlockSpec(blk, im)] * 7 + [pl.BlockSpec(pcb, im), pl.BlockSpec(blk, im)]
    return pl.pallas_call(
        functools.partial(_rwscan_kernel, nb=nb),
        grid=(bsz // nb, n_chunks),
        in_specs=dir_specs(fwd) + dir_specs(bwd) + [pl.BlockSpec((RW_STACK, GROUP_W), lambda b, i: (0, 0))],
        out_specs=[pl.BlockSpec(blk, fwd), pl.BlockSpec(blk, bwd)],
        out_shape=[jax.ShapeDtypeStruct((bsz, n, GROUP_W), f32)] * 2,
        scratch_shapes=[pltpu.VMEM((2, nb, GROUP_W, GROUP_W), f32)],
        compiler_params=_cparams(("parallel", "arbitrary")),
        name="rwscan",
    )(*prep[0:8], v, *prep[8:16], v, _head_mask().astype(bf16))


ATT_TQ = 256
ATT_KV_ROWS = 256


def _mla_kernel(pq_ref, pkv_ref, cos_ref, sin_ref, qg_ref, wqa_ref, wqb_ref, kvg_ref, wk_ref, wv_ref,
                e1_ref, e2_ref, o_ref, k_scr, v_scr, *, n_lat, n_all, n_q):
    qi = pl.program_id(1)
    tq = ATT_TQ
    hp = HEAD_PAD

    def tile4(t):
        return jnp.concatenate([t] * MLA_HEADS, axis=1)

    @pl.when(qi == 0)
    def _():
        def build(c, carry):
            r0 = pl.multiple_of(c * ATT_KV_ROWS, ATT_KV_ROWS)
            pkv = pkv_ref[0, pl.ds(r0, ATT_KV_ROWS), :]
            ckv = _rms(pkv[:, :MLA_KV_RANK], kvg_ref[...])
            kr = pkv[:, MLA_KV_RANK:]
            cos = tile4(cos_ref[pl.ds(r0, ATT_KV_ROWS), :])
            sin = tile4(sin_ref[pl.ds(r0, ATT_KV_ROWS), :])
            kmat = _dot(ckv, wk_ref[...]) + _dot(kr, e1_ref[...]) * cos + _dot(kr, e2_ref[...]) * sin
            k_scr[pl.ds(r0, ATT_KV_ROWS), :] = kmat.astype(bf16)
            v_scr[pl.ds(r0, ATT_KV_ROWS), :] = _dot(ckv, wv_ref[...]).astype(bf16)
            return carry

        lax.fori_loop(0, n_all // ATT_KV_ROWS, build, 0)

    r0 = pl.multiple_of(qi * tq, tq)
    qn = _rms(pq_ref[0], qg_ref[...]).astype(bf16)
    cos = tile4(cos_ref[pl.ds(r0, tq), :])
    sin = tile4(sin_ref[pl.ds(r0, tq), :])
    scale = float(MLA_NOPE + MLA_ROPE) ** -0.5
    q = (jnp.dot(qn, wqa_ref[...], preferred_element_type=f32) * cos
         + jnp.dot(qn, wqb_ref[...], preferred_element_type=f32) * sin) * scale
    q = q.astype(bf16)

    def attend(k0, k1):
        for pair in range(MLA_HEADS // 2):
            acc = jnp.zeros((tq, hp), f32)
            for h in (2 * pair, 2 * pair + 1):
                s = _dot_nt(q[:, h * hp:(h + 1) * hp], k_scr[k0:k1, h * hp:(h + 1) * hp])
                e = jnp.exp(s - jnp.max(s, axis=-1, keepdims=True))
                inv = 1.0 / jnp.sum(e, axis=-1, keepdims=True)
                acc = acc + _dot(e, v_scr[k0:k1, h * hp:(h + 1) * hp]) * inv
            o_ref[0, :, pair * hp:(pair + 1) * hp] = acc.astype(bf16)

    if n_q > n_lat:
        @pl.when(qi < n_lat // tq)
        def _():
            attend(0, n_all)

        @pl.when(qi >= n_lat // tq)
        def _():
            attend(n_lat, n_all)
    else:
        attend(0, n_all)


def _rope_rotation():
    quarter = MLA_ROPE // 4
    rot = np.zeros((MLA_ROPE, MLA_ROPE), np.float32)
    for axis in range(2):
        for f in range(quarter):
            first, second = axis * 2 * quarter + f, axis * 2 * quarter + quarter + f
            rot[second, first] = -1.0
            rot[first, second] = 1.0
    return rot


def _rope_tables(n_ctx, n_lat):
    rows = n_lat // GRID_W
    row = jnp.repeat(jnp.arange(rows), GRID_W).astype(f32)
    col = jnp.tile(jnp.arange(GRID_W), rows).astype(f32)
    n_freq = MLA_ROPE // 4
    freq = ROPE_BASE ** (-jnp.arange(n_freq, dtype=f32) / n_freq)
    ang = jnp.concatenate([row[:, None] * freq, row[:, None] * freq, col[:, None] * freq, col[:, None] * freq], 1)
    cos = jnp.ones((n_lat + n_ctx, HEAD_PAD), f32).at[:n_lat, MLA_NOPE:MLA_NOPE + MLA_ROPE].set(jnp.cos(ang))
    sin = jnp.zeros((n_lat + n_ctx, HEAD_PAD), f32).at[:n_lat, MLA_NOPE:MLA_NOPE + MLA_ROPE].set(jnp.sin(ang))
    return cos, sin


def _mla_weights(wuq, wukv):
    rot = _rope_rotation()
    src = np.argmax(np.abs(rot), axis=0)
    sign = jnp.asarray(rot[src, np.arange(MLA_ROPE)])
    hq = MLA_NOPE + MLA_ROPE
    hkv = MLA_NOPE + MLA_V
    wqa = jnp.zeros((wuq.shape[0], MLA_HEADS * HEAD_PAD), f32)
    wqb = jnp.zeros_like(wqa)
    wk = jnp.zeros((MLA_KV_RANK, MLA_HEADS * HEAD_PAD), f32)
    wv = jnp.zeros_like(wk)
    e1 = np.zeros((MLA_ROPE, MLA_HEADS * HEAD_PAD), np.float32)
    for h in range(MLA_HEADS):
        c0 = h * HEAD_PAD
        wqa = wqa.at[:, c0:c0 + hq].set(wuq[:, h * hq:(h + 1) * hq])
        wqb = wqb.at[:, c0 + MLA_NOPE:c0 + hq].set(wuq[:, h * hq + MLA_NOPE:(h + 1) * hq][:, src] * sign)
        wk = wk.at[:, c0:c0 + MLA_NOPE].set(wukv[:, h * hkv:h * hkv + MLA_NOPE])
        v0 = c0 + (h % 2) * MLA_V
        wv = wv.at[:, v0:v0 + MLA_V].set(wukv[:, h * hkv + MLA_NOPE:(h + 1) * hkv])
        e1[np.arange(MLA_ROPE), c0 + MLA_NOPE + np.arange(MLA_ROPE)] = 1.0
    place, place_rot = jnp.asarray(e1).astype(bf16), jnp.asarray(rot @ e1).astype(bf16)
    return wqa.astype(bf16), wqb.astype(bf16), wk.astype(bf16), wv.astype(bf16), place, place_rot


def _mla_call(p_q, p_kv, cos, sin, qn_g, wuq, kvn_g, wukv, n_lat, n_q):
    bsz, n, q_rank = p_q.shape
    wqa, wqb, wk, wv, e1, e2 = _mla_weights(wuq, wukv)
    hw = MLA_HEADS * HEAD_PAD
    full = lambda shape: pl.BlockSpec(shape, lambda b, i: tuple(0 for _ in shape))
    return pl.pallas_call(
        functools.partial(_mla_kernel, n_lat=n_lat, n_all=n, n_q=n_q),
        grid=(bsz, n_q // ATT_TQ),
        in_specs=[pl.BlockSpec((1, ATT_TQ, q_rank), lambda b, i: (b, i, 0)),
                  pl.BlockSpec((1, n, N_KV_IN), lambda b, i: (b, 0, 0)),
                  full((n, HEAD_PAD)), full((n, HEAD_PAD)),
                  full((1, q_rank)), full((q_rank, hw)), full((q_rank, hw)),
                  full((1, MLA_KV_RANK)), full((MLA_KV_RANK, hw)), full((MLA_KV_RANK, hw)),
                  full((MLA_ROPE, hw)), full((MLA_ROPE, hw))],
        out_specs=pl.BlockSpec((1, ATT_TQ, GROUP_W), lambda b, i: (b, i, 0)),
        out_shape=jax.ShapeDtypeStruct((bsz, n_q, GROUP_W), bf16),
        scratch_shapes=[pltpu.VMEM((n, hw), bf16), pltpu.VMEM((n, hw), bf16)],
        compiler_params=_cparams(("parallel", "arbitrary")),
        name="mla",
    )(p_q, p_kv, cos, sin, qn_g.reshape(1, -1), wqa, wqb, kvn_g.reshape(1, -1), wk, wv, e1, e2)


MLP_TF = 1024


def _tail_kernel(yc_ref, yp_ref, ya_ref, y0_ref, y1_ref, bonus_ref, gate_ref, x_ref, m_ref, lg_ref, lb_ref,
                 gavg_ref, g_ref, wo_ref, w1_ref, w2_ref, fg_ref, o_ref, x1_scr, h2_scr, acc_scr,
                 *, n_lat, n_rows, tm, final_norm):
    kf = pl.program_id(2)
    is_ctx = _is_ctx_rows(pl.program_id(1), tm, n_lat, n_rows)

    @pl.when(kf == 0)
    def _():
        o = _group_stats_norm(y0_ref[0] + y1_ref[0], gavg_ref[...], RW_LNX_EPS) * lg_ref[...] + lb_ref[...]
        y_rw = ((o + bonus_ref[0]) * gate_ref[0]).astype(bf16)
        y = jnp.concatenate([yc_ref[0], y_rw, yp_ref[0], ya_ref[0]], axis=1)
        x1 = x_ref[0] + _mod_row(m_ref, 2, is_ctx) * jnp.dot(y, wo_ref[...], preferred_element_type=f32)
        x1_scr[...] = x1
        h2 = _rms(x1, g_ref[...]) * (1.0 + _mod_row(m_ref, 4, is_ctx)) + _mod_row(m_ref, 3, is_ctx)
        h2_scr[...] = h2.astype(bf16)
        acc_scr[...] = jnp.zeros_like(acc_scr)

    z = jnp.maximum(jnp.dot(h2_scr[...], w1_ref[...], preferred_element_type=f32), 0.0)
    acc_scr[...] += jnp.dot((z * z).astype(bf16), w2_ref[...], preferred_element_type=f32)

    @pl.when(kf == pl.num_programs(2) - 1)
    def _():
        x2 = x1_scr[...] + _mod_row(m_ref, 5, is_ctx) * acc_scr[...]
        o_ref[0] = _rms(x2, fg_ref[...]) if final_norm else x2


def _tail_call(y_conv, y_pool, y_att, y0, y1, bonus, gate, xx, modall, lnx_g, lnx_b, g, w_out, w1, w2, final_g,
               n_lat, n_rows, final_norm):
    bsz, _, d = xx.shape
    dff = w1.shape[1]
    tm = next(t for t in (1024, 768, 512, 256) if n_rows % t == 0)
    row = lambda w: pl.BlockSpec((1, tm, w), lambda b, i, k: (b, i, 0))
    vec = lambda w: pl.BlockSpec((1, w), lambda b, i, k: (0, 0))
    return pl.pallas_call(
        functools.partial(_tail_kernel, n_lat=n_lat, n_rows=n_rows, tm=tm, final_norm=final_norm),
        grid=(bsz, n_rows // tm, dff // MLP_TF),
        in_specs=[row(GROUP_W)] * 7 + [row(d),
                  pl.BlockSpec((1, 12, d), lambda b, i, k: (b, 0, 0)),
                  vec(GROUP_W), vec(GROUP_W),
                  pl.BlockSpec((GROUP_W, GROUP_W), lambda b, i, k: (0, 0)),
                  vec(d),
                  pl.BlockSpec((d, d), lambda b, i, k: (0, 0)),
                  pl.BlockSpec((d, MLP_TF), lambda b, i, k: (0, k)),
                  pl.BlockSpec((MLP_TF, d), lambda b, i, k: (k, 0)),
                  vec(d)],
        out_specs=row(d),
        out_shape=jax.ShapeDtypeStruct((bsz, n_rows, d), f32),
        scratch_shapes=[pltpu.VMEM((tm, d), f32), pltpu.VMEM((tm, d), bf16), pltpu.VMEM((tm, d), f32)],
        compiler_params=_cparams(("parallel", "parallel", "arbitrary")),
        name="tail",
    )(y_conv, y_pool, y_att, y0, y1, bonus, gate, xx, modall, lnx_g.reshape(1, -1), lnx_b.reshape(1, -1),
      _group_avg_matrix(GROUP_W, RW_HEAD).astype(bf16), g.reshape(1, d), w_out.astype(bf16),
      w1.astype(bf16), w2.astype(bf16), final_g.reshape(1, d))


def kernel(x, c, ctx, c_ctx, ada_w, ada_b, norm1_g, norm2_g, w_in, w_out, conv_dw, conv_db, conv_gn_g, conv_gn_b, conv_pw, pool_w, pool_scale, rw_mu_prev, rw_mu_next, rw_w0, rw_w2, rw_a0, rw_a2, rw_g2, rw_kk, rw_ka, rw_rk, rw_lnx_g, rw_lnx_b, mla_qn_g, mla_wuq, mla_kvn_g, mla_wukv, mlp_w1, mlp_w2, final_g):
    bsz, n_lat, d = x.shape
    n_ctx = ctx.shape[1]
    depth = ada_w.shape[0]
    n_all = n_ctx + n_lat
    assert n_ctx % max(RW_TILE, ATT_TQ, CONV_ROWS) == 0 and n_lat % max(RW_TILE, ATT_TQ, CONV_ROWS) == 0
    assert n_lat % GRID_W == 0 and bsz + 1 <= 24

    xx = jnp.concatenate([x, ctx], axis=1)
    s_all = jnp.zeros((24, d), f32).at[:bsz].set(c).at[bsz].set(c_ctx)
    cos, sin = _rope_tables(n_ctx, n_lat)
    w_in_pad = jnp.pad(w_in, ((0, 0), (0, 0), (0, (-P_IN) % LANES))).astype(bf16)

    for l in range(depth):
        last = l == depth - 1
        mod = _ada_call(s_all, ada_w[l], ada_b[l])
        mod_lat = mod[:bsz].reshape(bsz, 6, d)
        mod_ctx = jnp.broadcast_to(mod[bsz].reshape(1, 6, d), (bsz, 6, d))
        modall = jnp.concatenate([mod_ctx, mod_lat], axis=1)

        n_rows = n_lat if last else n_all
        segs = ((0, n_lat),) if last else ((0, n_lat), (n_lat, n_ctx))
        p_conv, p_pool, p_q, p_rw, p_kv = _inproj_call(xx, modall, norm1_g[l], w_in_pad[l], n_lat)
        y_conv = _conv_call(p_conv, conv_dw[l], conv_db[l], conv_gn_g[l], conv_gn_b[l], conv_pw[l], segs)
        y_pool = _pool_call(p_pool, pool_w[l], pool_scale[l], segs)
        prep = _rwprep_call(p_rw, rw_mu_prev[l], rw_mu_next[l], rw_w0[l], rw_w2[l], rw_a0[l], rw_a2[l],
                            rw_g2[l], rw_kk[l], rw_ka[l], rw_rk[l], n_lat)
        y0, y1 = _rwscan_call(prep, n_ctx)
        y_att = _mla_call(p_q, p_kv, cos, sin, mla_qn_g[l], mla_wuq[l], mla_kvn_g[l], mla_wukv[l], n_lat, n_rows)
        xx = _tail_call(y_conv, y_pool, y_att, y0, y1, prep[17], prep[18], xx, modall, rw_lnx_g[l], rw_lnx_b[l],
                        norm2_g[l], w_out[l], mlp_w1[l], mlp_w2[l], final_g, n_lat, n_rows, last)
    return xx
```

```python
import functools

import numpy as np
import jax
import jax.numpy as jnp
from jax import lax
from jax.experimental import pallas as pl
from jax.experimental.pallas import tpu as pltpu

f32 = jnp.float32
bf16 = jnp.bfloat16
HIGHEST = lax.Precision.HIGHEST

GROUP_W = 256
NORM_EPS = 1e-6
GN_EPS = 1e-5
CONV_K = 31
CONV_NORM_GROUPS = 4
RW_HEAD = 64
RW_HEADS = GROUP_W // RW_HEAD
RW_DECAY_LORA = 64
RW_A_LORA = 64
RW_GATE_LORA = 128
RW_LNX_EPS = 64e-5
POOL_WINDOWS = (2, 4, 8, 16)
MLA_HEADS = 4
MLA_NOPE = 64
MLA_ROPE = 32
MLA_V = 64
MLA_KV_RANK = 128
ROPE_BASE = 10000.0
GRID_W = 64
N_CONV_IN = 2 * GROUP_W
N_RW_IN = 3 * GROUP_W + RW_GATE_LORA + 2 * RW_DECAY_LORA + 2 * RW_A_LORA
N_KV_IN = MLA_KV_RANK + MLA_ROPE
OFF_POOL = N_CONV_IN
OFF_Q = OFF_POOL + GROUP_W
OFF_RW = OFF_Q + GROUP_W
OFF_KV = OFF_RW + N_RW_IN
P_IN = OFF_KV + N_KV_IN

LANES = 128
SUBLANES = 8
VMEM_LIMIT_BYTES = 56 * 1024 * 1024

RW_CHUNK = 64
HEAD_PAD = 128


def _cparams(sem):
    return pltpu.CompilerParams(dimension_semantics=sem, vmem_limit_bytes=VMEM_LIMIT_BYTES)


def _dot(a, b):
    return jnp.dot(a.astype(bf16), b.astype(bf16), preferred_element_type=f32)


def _dot_hi(a, b):
    return jnp.dot(a, b, preferred_element_type=f32, precision=HIGHEST)


def _dot_nt(a, b):
    return lax.dot_general(a.astype(bf16), b.astype(bf16), (((1,), (1,)), ((), ())), preferred_element_type=f32)


def _dot_tn(a, b):
    return lax.dot_general(a.astype(bf16), b.astype(bf16), (((0,), (0,)), ((), ())), preferred_element_type=f32)


def _dot_split(a, b, terms):
    split_lhs = a.dtype == f32
    x = a if split_lhs else b
    acc = None
    for _ in range(terms):
        piece = x.astype(bf16)
        part = jnp.dot(piece if split_lhs else a, b if split_lhs else piece, preferred_element_type=f32)
        acc = part if acc is None else acc + part
        x = x - piece.astype(f32)
    return acc


def _dot_3pass(a, b):
    a_hi, b_hi = a.astype(bf16), b.astype(bf16)
    a_lo, b_lo = (a - a_hi.astype(f32)).astype(bf16), (b - b_hi.astype(f32)).astype(bf16)
    return (jnp.dot(a_hi, b_hi, preferred_element_type=f32) + jnp.dot(a_lo, b_hi, preferred_element_type=f32)
            + jnp.dot(a_hi, b_lo, preferred_element_type=f32))


def _sigmoid(x):
    return jax.nn.sigmoid(x)


def _mod_row(m_ref, j, is_ctx):
    lat = m_ref[0, 6 + j:7 + j, :]
    if is_ctx is None:
        return lat
    return jnp.where(is_ctx, m_ref[0, j:j + 1, :], lat)


def _is_ctx_rows(tile_idx, tm, n_lat, n_rows):
    if n_rows <= n_lat:
        return None
    row = tile_idx * tm + lax.broadcasted_iota(jnp.int32, (tm, 1), 0)
    return row >= n_lat


def _group_stats_norm(y, gavg, eps):
    dlt = y - _dot_split(y, gavg, 2)
    return dlt * lax.rsqrt(_dot_split(dlt * dlt, gavg, 2) + eps)


def _rms(x, g):
    return x * lax.rsqrt(jnp.mean(x * x, axis=-1, keepdims=True) + NORM_EPS) * g


def _ada_kernel(s_ref, w_ref, b_ref, o_ref):
    s = s_ref[...]
    s = s * _sigmoid(s)
    o_ref[...] = _dot_hi(s, w_ref[...]) + b_ref[...]


def _ada_call(s_all, w, b, layer):
    rows, d = s_all.shape
    depth, _, n = w.shape
    tn = 1536
    return pl.pallas_call(
        _ada_kernel,
        grid=(n // tn,),
        in_specs=[pl.BlockSpec((rows, d), lambda j: (0, 0)),
                  pl.BlockSpec((None, d, tn), lambda j: (layer, 0, j)),
                  pl.BlockSpec((None, 1, tn), lambda j: (layer, 0, j))],
        out_specs=pl.BlockSpec((rows, tn), lambda j: (0, j)),
        out_shape=jax.ShapeDtypeStruct((rows, n), f32),
        compiler_params=_cparams(("arbitrary",)),
        name="ada",
    )(s_all, w, b.reshape(depth, 1, n))


def _inproj_kernel(x_ref, m_ref, g_ref, w_ref, oc_ref, op_ref, oq_ref, orw_ref, okv_ref, *, n_lat, n_rows, tm):
    is_ctx = _is_ctx_rows(pl.program_id(1), tm, n_lat, n_rows)
    h = _rms(x_ref[0], g_ref[...]) * (1.0 + _mod_row(m_ref, 1, is_ctx)) + _mod_row(m_ref, 0, is_ctx)
    p = jnp.dot(h.astype(bf16), w_ref[...], preferred_element_type=f32)
    oc_ref[0] = p[:, 0:OFF_POOL]
    op_ref[0] = p[:, OFF_POOL:OFF_Q]
    oq_ref[0] = p[:, OFF_Q:OFF_RW]
    orw_ref[0] = p[:, OFF_RW:OFF_KV]
    okv_ref[0] = p[:, OFF_KV:P_IN]


def _inproj_call(xx, modall, g, w_pad, layer, n_lat):
    bsz, n, d = xx.shape
    tm = 384 if n % 384 == 0 else 256
    widths = (N_CONV_IN, GROUP_W, GROUP_W, N_RW_IN, N_KV_IN)
    return pl.pallas_call(
        functools.partial(_inproj_kernel, n_lat=n_lat, n_rows=n, tm=tm),
        grid=(bsz, n // tm),
        in_specs=[pl.BlockSpec((1, tm, d), lambda b, i: (b, i, 0)),
                  pl.BlockSpec((1, 12, d), lambda b, i: (b, 0, 0)),
                  pl.BlockSpec((1, d), lambda b, i: (0, 0)),
                  pl.BlockSpec((None,) + w_pad.shape[1:], lambda b, i: (layer, 0, 0))],
        out_specs=[pl.BlockSpec((1, tm, w), lambda b, i: (b, i, 0)) for w in widths],
        out_shape=[jax.ShapeDtypeStruct((bsz, n, w), f32) for w in widths],
        compiler_params=_cparams(("parallel", "parallel")),
        name="inproj",
    )(xx, modall, g.reshape(1, d), w_pad)


CONV_ROWS = 128
CONV_HALO = 16


def _conv_kernel(p_ref, dw_ref, db_ref, gg_ref, gb_ref, pw_ref, gavg_ref, o_ref, u_scr, *, segs):
    r, hl = CONV_ROWS, CONV_HALO
    win_rows = r + 2 * hl
    for s0, n in segs:
        u_scr[0:hl, :] = jnp.zeros((hl, GROUP_W), f32)
        u_scr[hl + n:hl + n + hl, :] = jnp.zeros((hl, GROUP_W), f32)

        def fill(c, carry, s0=s0):
            r0 = pl.multiple_of(c * r, r)
            blk = p_ref[0, pl.ds(s0 + r0, r), :]
            u_scr[pl.ds(hl + r0, r), :] = blk[:, :GROUP_W] * _sigmoid(blk[:, GROUP_W:])
            return carry

        lax.fori_loop(0, n // r, fill, 0)

        def body(c, carry, s0=s0):
            r0 = pl.multiple_of(c * r, r)
            win = u_scr[pl.ds(r0, win_rows), :]
            rolled = [win] + [pltpu.roll(win, win_rows - b, axis=0) for b in range(1, SUBLANES)]
            acc = jnp.zeros((r, GROUP_W), f32)
            for j in range(CONV_K):
                off = hl - CONV_K // 2 + j
                base = off - off % SUBLANES
                acc = acc + rolled[off % SUBLANES][base:base + r] * dw_ref[j:j + 1, :]
            yn = _group_stats_norm(acc + db_ref[...], gavg_ref[...], GN_EPS) * gg_ref[...] + gb_ref[...]
            act = yn * _sigmoid(yn)
            o_ref[0, pl.ds(s0 + r0, r), :] = _dot(act, pw_ref[...]).astype(bf16)
            return carry

        lax.fori_loop(0, n // r, body, 0)


def _group_avg_matrix(width, group):
    idx = np.arange(width) // group
    return jnp.asarray((idx[:, None] == idx[None, :]).astype(np.float32) / group)


def _conv_call(p_conv, dw, db, gg, gb, pw, segs):
    bsz, n, _ = p_conv.shape
    n_out = max(s[0] + s[1] for s in segs)
    gavg = _group_avg_matrix(GROUP_W, GROUP_W // CONV_NORM_GROUPS).astype(bf16)
    max_seg = max(s[1] for s in segs)
    vec = lambda: pl.BlockSpec((1, GROUP_W), lambda b: (0, 0))
    return pl.pallas_call(
        functools.partial(_conv_kernel, segs=segs),
        grid=(bsz,),
        in_specs=[pl.BlockSpec((1, n, N_CONV_IN), lambda b: (b, 0, 0)),
                  pl.BlockSpec((CONV_K, GROUP_W), lambda b: (0, 0)),
                  vec(), vec(), vec(),
                  pl.BlockSpec((GROUP_W, GROUP_W), lambda b: (0, 0)),
                  pl.BlockSpec((GROUP_W, GROUP_W), lambda b: (0, 0))],
        out_specs=pl.BlockSpec((1, n_out, GROUP_W), lambda b: (b, 0, 0)),
        out_shape=jax.ShapeDtypeStruct((bsz, n_out, GROUP_W), bf16),
        scratch_shapes=[pltpu.VMEM((max_seg + 2 * CONV_HALO, GROUP_W), f32)],
        compiler_params=_cparams(("parallel",)),
        name="conv",
    )(p_conv, dw, db.reshape(1, -1), gg.reshape(1, -1), gb.reshape(1, -1), pw.astype(bf16), gavg)


def _pool_kernel(p_ref, w_ref, sc_ref, o_ref, u_scr, *, segs):
    r, hl = CONV_ROWS, CONV_HALO
    win_rows = r + 2 * hl
    pool_ch = GROUP_W // len(POOL_WINDOWS)
    lane = lax.broadcasted_iota(jnp.int32, (1, GROUP_W), 1)
    half = jnp.full((1, GROUP_W), POOL_WINDOWS[-1] // 2, jnp.int32)
    for gi in range(len(POOL_WINDOWS) - 2, -1, -1):
        half = jnp.where(lane < (gi + 1) * pool_ch, POOL_WINDOWS[gi] // 2, half)

    def shifted(v, k):
        return pltpu.roll(v, (win_rows - k) % win_rows, axis=0)

    for s0, n in segs:
        u_scr[0:hl, :] = jnp.zeros((hl, GROUP_W), f32)
        u_scr[hl + n:hl + n + hl, :] = jnp.zeros((hl, GROUP_W), f32)

        def fill(c, carry, s0=s0):
            r0 = pl.multiple_of(c * r, r)
            u_scr[pl.ds(hl + r0, r), :] = p_ref[0, pl.ds(s0 + r0, r), :]
            return carry

        lax.fori_loop(0, n // r, fill, 0)

        def body(c, carry, s0=s0, n=n):
            r0 = pl.multiple_of(c * r, r)
            win = u_scr[pl.ds(r0, win_rows), :]
            s2 = win + shifted(win, -1)
            s4 = shifted(s2, -1) + shifted(s2, 1)
            s8 = shifted(s4, -2) + shifted(s4, 2)
            s16 = shifted(s8, -4) + shifted(s8, 4)
            sums = (s2, s4, s8, s16)
            sel = sums[-1]
            for gi in range(len(POOL_WINDOWS) - 2, -1, -1):
                sel = jnp.where(lane < (gi + 1) * pool_ch, sums[gi], sel)
            sel = sel[hl:hl + r]
            u = win[hl:hl + r]
            t = r0 + lax.broadcasted_iota(jnp.int32, (r, 1), 0)
            cnt = (jnp.minimum(t + half, n) - jnp.maximum(t - half, 0)).astype(f32)
            dlt = sel / cnt - u
            o_ref[0, pl.ds(s0 + r0, r), :] = (_dot(dlt, w_ref[...]) * sc_ref[...]).astype(bf16)
            return carry

        lax.fori_loop(0, n // r, body, 0)


def _block_diag(blocks):
    g, a, b = blocks.shape
    out = jnp.zeros((g * a, g * b), blocks.dtype)
    for i in range(g):
        out = out.at[i * a:(i + 1) * a, i * b:(i + 1) * b].set(blocks[i])
    return out


def _pool_call(p_pool, pool_w, pool_scale, segs):
    bsz, n, _ = p_pool.shape
    n_out = max(s[0] + s[1] for s in segs)
    max_seg = max(s[1] for s in segs)
    return pl.pallas_call(
        functools.partial(_pool_kernel, segs=segs),
        grid=(bsz,),
        in_specs=[pl.BlockSpec((1, n, GROUP_W), lambda b: (b, 0, 0)),
                  pl.BlockSpec((GROUP_W, GROUP_W), lambda b: (0, 0)),
                  pl.BlockSpec((1, GROUP_W), lambda b: (0, 0))],
        out_specs=pl.BlockSpec((1, n_out, GROUP_W), lambda b: (b, 0, 0)),
        out_shape=jax.ShapeDtypeStruct((bsz, n_out, GROUP_W), bf16),
        scratch_shapes=[pltpu.VMEM((max_seg + 2 * CONV_HALO, GROUP_W), f32)],
        compiler_params=_cparams(("parallel",)),
        name="pool",
    )(p_pool, _block_diag(pool_w).astype(bf16), pool_scale.reshape(1, -1))


RW_TILE = 256
RW_R, RW_K, RW_V, RW_G = 0, GROUP_W, 2 * GROUP_W, 3 * GROUP_W
RW_W = RW_G + RW_GATE_LORA
RW_A = RW_W + 2 * RW_DECAY_LORA
RW_STACK = RW_HEADS * RW_CHUNK


def _stack_heads(x, hm):
    return jnp.concatenate([x] * RW_HEADS, axis=0) * hm


def _unstack_heads(z):
    c = z.shape[0] // RW_HEADS
    out = z[0:c]
    for h in range(1, RW_HEADS):
        out = out + z[h * c:(h + 1) * c]
    return out


def _scan_order_masks():
    ri = lax.broadcasted_iota(jnp.int32, (RW_STACK, RW_STACK), 0)
    ci = lax.broadcasted_iota(jnp.int32, (RW_STACK, RW_STACK), 1)
    return ((ri > ci, ri >= ci), (ri < ci, ri <= ci)), (ri == ci).astype(f32)


def _head_mask():
    m = np.arange(RW_STACK)[:, None] // RW_CHUNK == np.arange(GROUP_W)[None, :] // RW_HEAD
    return jnp.asarray(m.astype(np.float32))


def _rwprep_kernel(p_ref, hp_ref, hn_ref, mup_ref, mun_ref, w0_ref, w2_ref, a0_ref, a2_ref, g2_ref,
                   kk_ref, ka_ref, rk_ref, tril_ref, triu_ref, hsum_ref, hm_ref,
                   rt0, bt0, bp0, kp0, kh0, wc0, yp0, pc0, rt1, bt1, bp1, kp1, kh1, wc1, yp1, pc1,
                   v_out, bonus_out, gate_out, *, seg_starts, seg_ends):
    tr, c = RW_TILE, RW_CHUNK
    i = pl.program_id(1)
    row0 = i * tr
    first = functools.reduce(jnp.logical_or, [row0 == s for s in seg_starts])
    last = functools.reduce(jnp.logical_or, [row0 + tr == e for e in seg_ends])
    ridx = lax.broadcasted_iota(jnp.int32, (tr, 1), 0)

    def zcols(a, b):
        p = p_ref[0, :, a:b]
        prev_row = jnp.where(first, 0.0, hp_ref[0, SUBLANES - 1:SUBLANES, a:b])
        next_row = jnp.where(last, 0.0, hn_ref[0, 0:1, a:b])
        prev = jnp.where(ridx == 0, prev_row, pltpu.roll(p, 1, axis=0))
        nxt = jnp.where(ridx == tr - 1, next_row, pltpu.roll(p, tr - 1, axis=0))
        return p + mup_ref[:, a:b] * (prev - p) + mun_ref[:, a:b] * (nxt - p)

    r = zcols(RW_R, RW_K)
    k = zcols(RW_K, RW_V)
    v = zcols(RW_V, RW_G)
    vb = v.astype(bf16)
    v_out[0] = vb
    gate_out[0] = _dot(_sigmoid(zcols(RW_G, RW_W)), g2_ref[...])
    kk = k * kk_ref[...]
    kk = kk / jnp.maximum(jnp.sqrt(_dot_split(kk * kk, hsum_ref[...], 2)), 1e-12)
    w_all = _dot_3pass(jnp.tanh(zcols(RW_W, RW_A)), w2_ref[...]) + w0_ref[...]
    a_all = _sigmoid(_dot_3pass(zcols(RW_A, N_RW_IN), a2_ref[...]) + a0_ref[...])
    kd_sum = jnp.zeros_like(k)
    outs = ((rt0, bt0, bp0, kp0, kh0, wc0, yp0, pc0, tril_ref), (rt1, bt1, bp1, kp1, kh1, wc1, yp1, pc1, triu_ref))
    scaled = []
    for d, (rt_o, bt_o, bp_o, kp_o, _, _, _, pc_o, tri_ref) in enumerate(outs):
        x = w_all[:, d * GROUP_W:(d + 1) * GROUP_W]
        neg = -x
        log_w = -(jnp.maximum(neg, 0.0) + jnp.log1p(jnp.exp(-jnp.abs(neg)))) - 0.5
        lw = -jnp.exp(log_w)
        a = a_all[:, d * GROUP_W:(d + 1) * GROUP_W]
        kd = k * (1.0 + (a - 1.0) * ka_ref[...])
        kd_sum = kd_sum + kd
        b = kk * a
        cum = _dot_split(tri_ref[...], lw, 3)
        tot_rows = []
        for ci in range(tr // c):
            edge = ci * c + (c - 1 if d == 0 else 0)
            tot_rows.append(cum[edge:edge + 1, :])
            pc_o[0, ci * SUBLANES:(ci + 1) * SUBLANES, :] = jnp.broadcast_to(jnp.exp(tot_rows[-1]), (SUBLANES, GROUP_W))
        tot = jnp.concatenate([jnp.broadcast_to(t, (c, GROUP_W)) for t in tot_rows], axis=0)
        e_neg = jnp.exp(-cum)
        e_rest = jnp.exp(tot - cum)
        rt = (r * jnp.exp(cum)).astype(bf16)
        kq = (kk * jnp.exp(cum - lw)).astype(bf16)
        bt = (b * e_neg).astype(bf16)
        kt = (kd * e_neg).astype(bf16)
        rt_o[0] = rt
        bt_o[0] = bt
        bp_o[0] = (b * e_rest).astype(bf16)
        kp_o[0] = (kd * e_rest).astype(bf16)
        scaled.append((rt, kq, bt, kt))
    bonus_out[0] = _dot_split(r * rk_ref[...] * kd_sum, hsum_ref[...], 2) * v

    hm = hm_ref[...]
    masks, eye = _scan_order_masks()
    chains = [(d, ci) for d in range(2) for ci in range(tr // c)]
    rows = lambda ci: slice(ci * c, (ci + 1) * c)
    rs = [_stack_heads(scaled[d][0][rows(ci)], hm) for d, ci in chains]
    ks = [_stack_heads(scaled[d][1][rows(ci)], hm) for d, ci in chains]
    bs = [_stack_heads(scaled[d][2][rows(ci)], hm) for d, ci in chains]
    kts = [_stack_heads(scaled[d][3][rows(ci)], hm) for d, ci in chains]
    vs = [_stack_heads(vb[rows(ci)], hm) for d, ci in chains]
    a_ub = [jnp.where(masks[d][0], _dot_nt(ks[q], bs[q]), 0.0) for q, (d, ci) in enumerate(chains)]
    a_vk = [jnp.where(masks[d][0], _dot_nt(ks[q], kts[q]), 0.0) for q, (d, ci) in enumerate(chains)]
    a_rk = [jnp.where(masks[d][1], _dot_nt(rs[q], kts[q]), 0.0) for q, (d, ci) in enumerate(chains)]
    tinv = [eye - a for a in a_ub]
    apow = a_ub
    for _ in range(int(np.log2(c)) - 1):
        apow = [_dot(a, a) for a in apow]
        tinv = [t + _dot(t, a) for t, a in zip(tinv, apow)]
    for q, (d, ci) in enumerate(chains):
        kh_o, wc_o, yp_o = outs[d][4], outs[d][5], outs[d][6]
        kh_o[0, rows(ci), :] = _unstack_heads(_dot(tinv[q], ks[q])).astype(bf16)
        wc_o[0, rows(ci), :] = _unstack_heads(_dot(tinv[q], _dot(a_vk[q], vs[q])))
        yp_o[0, rows(ci), :] = _unstack_heads(_dot(a_rk[q], vs[q]))


def _chunk_tri(tile, chunk, upper):
    t = np.arange(tile)
    same = (t[:, None] // chunk) == (t[None, :] // chunk)
    tri = (t[None, :] >= t[:, None]) if upper else (t[None, :] <= t[:, None])
    return jnp.asarray((same & tri).astype(np.float32))


def _rwprep_call(p_rw, mu_prev, mu_next, w0, w2, a0, a2, g2, kkp, kap, rk, n_lat):
    bsz, n, _ = p_rw.shape
    tr = RW_TILE
    nb8 = n // SUBLANES
    seg_starts = tuple(sorted({0, n_lat} - {n}))
    seg_ends = tuple(sorted({n_lat, n}))
    zeros = jnp.zeros((RW_DECAY_LORA, GROUP_W), f32)
    w2cat = jnp.concatenate([jnp.concatenate([w2[0], zeros], 1), jnp.concatenate([zeros, w2[1]], 1)], 0)
    a2cat = jnp.concatenate([jnp.concatenate([a2[0], zeros], 1), jnp.concatenate([zeros, a2[1]], 1)], 0)
    hsum = _group_avg_matrix(GROUP_W, RW_HEAD) * RW_HEAD
    full = lambda shape: pl.BlockSpec(shape, lambda b, i: tuple(0 for _ in shape))
    row_spec = pl.BlockSpec((1, tr, GROUP_W), lambda b, i: (b, i, 0))
    pc_rows = tr // RW_CHUNK * SUBLANES
    pc_spec = pl.BlockSpec((1, pc_rows, GROUP_W), lambda b, i: (b, i, 0))
    arr = lambda dt: jax.ShapeDtypeStruct((bsz, n, GROUP_W), dt)
    pc_arr = jax.ShapeDtypeStruct((bsz, n // RW_CHUNK * SUBLANES, GROUP_W), f32)
    dir_specs = [row_spec] * 7 + [pc_spec]
    dir_shapes = [arr(bf16)] * 5 + [arr(f32), arr(f32), pc_arr]
    return pl.pallas_call(
        functools.partial(_rwprep_kernel, seg_starts=seg_starts, seg_ends=seg_ends),
        grid=(bsz, n // tr),
        in_specs=[pl.BlockSpec((1, tr, N_RW_IN), lambda b, i: (b, i, 0)),
                  pl.BlockSpec((1, SUBLANES, N_RW_IN),
                               lambda b, i: (b, jnp.maximum(i * (tr // SUBLANES) - 1, 0), 0)),
                  pl.BlockSpec((1, SUBLANES, N_RW_IN),
                               lambda b, i: (b, jnp.minimum((i + 1) * (tr // SUBLANES), nb8 - 1), 0)),
                  full((1, N_RW_IN)), full((1, N_RW_IN)),
                  full((1, 2 * GROUP_W)), full((2 * RW_DECAY_LORA, 2 * GROUP_W)),
                  full((1, 2 * GROUP_W)), full((2 * RW_A_LORA, 2 * GROUP_W)),
                  full((RW_GATE_LORA, GROUP_W)),
                  full((1, GROUP_W)), full((1, GROUP_W)), full((1, GROUP_W)),
                  full((tr, tr)), full((tr, tr)), full((GROUP_W, GROUP_W)), full((RW_STACK, GROUP_W))],
        out_specs=dir_specs * 2 + [row_spec] * 3,
        out_shape=dir_shapes * 2 + [arr(bf16), arr(f32), arr(f32)],
        compiler_params=_cparams(("parallel", "parallel")),
        name="rwprep",
    )(p_rw, p_rw, p_rw, mu_prev.reshape(1, -1), mu_next.reshape(1, -1),
      w0.reshape(1, -1), w2cat, a0.reshape(1, -1), a2cat, g2.astype(bf16),
      kkp.reshape(1, -1), kap.reshape(1, -1), rk.reshape(1, -1),
      _chunk_tri(tr, RW_CHUNK, False).astype(bf16), _chunk_tri(tr, RW_CHUNK, True).astype(bf16),
      hsum.astype(bf16), _head_mask().astype(bf16))


RW_SCAN_BATCH = 8


def _rwscan_kernel(*refs, nb):
    nd = 9
    dir_refs = (refs[0:nd], refs[nd:2 * nd])
    hm_ref, y_refs, s_scr = refs[2 * nd], refs[2 * nd + 1:2 * nd + 3], refs[2 * nd + 3]

    @pl.when(pl.program_id(1) == 0)
    def _():
        s_scr[...] = jnp.zeros_like(s_scr)

    hm = hm_ref[...]
    hm32 = hm.astype(f32)
    masks, _ = _scan_order_masks()
    chains = [(d, bb) for bb in range(nb) for d in range(2)]
    ld = lambda d, bb, k: dir_refs[d][k][bb]
    rs = [_stack_heads(ld(d, bb, 0), hm) for d, bb in chains]
    bs = [_stack_heads(ld(d, bb, 1), hm) for d, bb in chains]
    bps = [_stack_heads(ld(d, bb, 2), hm) for d, bb in chains]
    kps = [_stack_heads(ld(d, bb, 3), hm) for d, bb in chains]
    khs = [_stack_heads(ld(d, bb, 4), hm) for d, bb in chains]
    w2t = [_stack_heads(ld(d, bb, 5), hm32).T for d, bb in chains]
    vs = [_stack_heads(ld(d, bb, 8), hm) for d, bb in chains]
    s = [s_scr[d, bb] for d, bb in chains]
    sb = [x.astype(bf16) for x in s]
    us_t = [-(_dot_nt(sb[j], khs[j]) + w2t[j]) for j in range(len(chains))]
    a_rb = [jnp.where(masks[d][1], _dot_nt(rs[j], bs[j]), 0.0) for j, (d, bb) in enumerate(chains)]
    ds = [_dot(us_t[j], bps[j]) + _dot_tn(vs[j], kps[j]) for j in range(len(chains))]
    for j, (d, bb) in enumerate(chains):
        pc_rows = jnp.concatenate([ld(d, bb, 7)] * (GROUP_W // SUBLANES), axis=0)
        s_scr[d, bb] = s[j] * pc_rows + ds[j]
    ys = [_dot_nt(rs[j], sb[j]) + _dot(a_rb[j], us_t[j].T) for j in range(len(chains))]
    for j, (d, bb) in enumerate(chains):
        y_refs[d][bb] = _unstack_heads(ys[j]) + ld(d, bb, 6)


def _rwscan_call(prep, n_ctx):
    v = prep[16]
    bsz, n, _ = v.shape
    c = RW_CHUNK
    nb = max(t for t in range(1, RW_SCAN_BATCH + 1) if bsz % t == 0)
    n_chunks = n // c
    ctx_chunks = n_ctx // c
    lat_chunks = n_chunks - ctx_chunks

    def fwd(b, i):
        return (b, jnp.where(i < ctx_chunks, lat_chunks + i, i - ctx_chunks), 0)

    def bwd(b, i):
        return (b, n_chunks - 1 - i, 0)

    blk = (nb, c, GROUP_W)
    pcb = (nb, SUBLANES, GROUP_W)
    dir_specs = lambda im: [pl.BlockSpec(blk, im)] * 7 + [pl.BlockSpec(pcb, im), pl.BlockSpec(blk, im)]
    return pl.pallas_call(
        functools.partial(_rwscan_kernel, nb=nb),
        grid=(bsz // nb, n_chunks),
        in_specs=dir_specs(fwd) + dir_specs(bwd) + [pl.BlockSpec((RW_STACK, GROUP_W), lambda b, i: (0, 0))],
        out_specs=[pl.BlockSpec(blk, fwd), pl.BlockSpec(blk, bwd)],
        out_shape=[jax.ShapeDtypeStruct((bsz, n, GROUP_W), f32)] * 2,
        scratch_shapes=[pltpu.VMEM((2, nb, GROUP_W, GROUP_W), f32)],
        compiler_params=_cparams(("parallel", "arbitrary")),
        name="rwscan",
    )(*prep[0:8], v, *prep[8:16], v, _head_mask().astype(bf16))


ATT_TQ = 256
ATT_KV_ROWS = 256


def _mla_kernel(pq_ref, pkv_ref, cos_ref, sin_ref, qg_ref, wqa_ref, wqb_ref, kvg_ref, wk_ref, wv_ref,
                e1_ref, e2_ref, o_ref, k_scr, vt_scr, *, n_lat, n_all, n_q):
    qi = pl.program_id(1)
    tq = ATT_TQ
    hp = HEAD_PAD

    def tile4(t):
        return jnp.concatenate([t] * MLA_HEADS, axis=1)

    @pl.when(qi == 0)
    def _():
        for r0 in range(0, n_all, ATT_KV_ROWS):
            pkv = pkv_ref[0, r0:r0 + ATT_KV_ROWS, :]
            ckv = _rms(pkv[:, :MLA_KV_RANK], kvg_ref[...])
            kr = pkv[:, MLA_KV_RANK:]
            cos = tile4(cos_ref[r0:r0 + ATT_KV_ROWS, :])
            sin = tile4(sin_ref[r0:r0 + ATT_KV_ROWS, :])
            kmat = _dot(ckv, wk_ref[...]) + _dot(kr, e1_ref[...]) * cos + _dot(kr, e2_ref[...]) * sin
            k_scr[r0:r0 + ATT_KV_ROWS, :] = kmat.astype(bf16)
            vt_scr[:, r0:r0 + ATT_KV_ROWS] = _dot(ckv, wv_ref[...]).T.astype(bf16)

    r0 = pl.multiple_of(qi * tq, tq)
    qn = _rms(pq_ref[0], qg_ref[...]).astype(bf16)
    cos = tile4(cos_ref[pl.ds(r0, tq), :])
    sin = tile4(sin_ref[pl.ds(r0, tq), :])
    scale = float(MLA_NOPE + MLA_ROPE) ** -0.5
    q = (jnp.dot(qn, wqa_ref[...], preferred_element_type=f32) * cos
         + jnp.dot(qn, wqb_ref[...], preferred_element_type=f32) * sin) * scale
    q = q.astype(bf16)

    def attend(k0, k1):
        heads = range(MLA_HEADS)
        st = [_dot_nt(k_scr[k0:k1, h * hp:(h + 1) * hp], q[:, h * hp:(h + 1) * hp]) for h in heads]
        e = [jnp.exp(st[h] - jnp.max(st[h], axis=0, keepdims=True)) for h in heads]
        inv = [1.0 / jnp.sum(e[h], axis=0, keepdims=True) for h in heads]
        outs = [_dot(vt_scr[h * MLA_V:(h + 1) * MLA_V, k0:k1], e[h]) * inv[h] for h in heads]
        o_ref[0] = jnp.concatenate(outs, axis=0).T.astype(bf16)

    if n_q > n_lat:
        @pl.when(qi < n_lat // tq)
        def _():
            attend(0, n_all)

        @pl.when(qi >= n_lat // tq)
        def _():
            attend(n_lat, n_all)
    else:
        attend(0, n_all)


def _rope_rotation():
    quarter = MLA_ROPE // 4
    rot = np.zeros((MLA_ROPE, MLA_ROPE), np.float32)
    for axis in range(2):
        for f in range(quarter):
            first, second = axis * 2 * quarter + f, axis * 2 * quarter + quarter + f
            rot[second, first] = -1.0
            rot[first, second] = 1.0
    return rot


def _rope_tables(n_ctx, n_lat):
    rows = n_lat // GRID_W
    row = jnp.repeat(jnp.arange(rows), GRID_W).astype(f32)
    col = jnp.tile(jnp.arange(GRID_W), rows).astype(f32)
    n_freq = MLA_ROPE // 4
    freq = ROPE_BASE ** (-jnp.arange(n_freq, dtype=f32) / n_freq)
    ang = jnp.concatenate([row[:, None] * freq, row[:, None] * freq, col[:, None] * freq, col[:, None] * freq], 1)
    cos = jnp.ones((n_lat + n_ctx, HEAD_PAD), f32).at[:n_lat, MLA_NOPE:MLA_NOPE + MLA_ROPE].set(jnp.cos(ang))
    sin = jnp.zeros((n_lat + n_ctx, HEAD_PAD), f32).at[:n_lat, MLA_NOPE:MLA_NOPE + MLA_ROPE].set(jnp.sin(ang))
    return cos, sin


def _mla_weights(wuq, wukv):
    rot = _rope_rotation()
    hq = MLA_NOPE + MLA_ROPE
    hkv = MLA_NOPE + MLA_V
    hw = MLA_HEADS * HEAD_PAD
    pqa = np.zeros((MLA_HEADS * hq, hw), np.float32)
    pqb = np.zeros_like(pqa)
    pk = np.zeros((MLA_HEADS * hkv, hw), np.float32)
    pv = np.zeros((MLA_HEADS * hkv, MLA_HEADS * MLA_V), np.float32)
    e1 = np.zeros((MLA_ROPE, hw), np.float32)
    for h in range(MLA_HEADS):
        c0 = h * HEAD_PAD
        pqa[h * hq + np.arange(hq), c0 + np.arange(hq)] = 1.0
        pqb[h * hq + MLA_NOPE:(h + 1) * hq, c0 + MLA_NOPE:c0 + hq] = rot
        pk[h * hkv + np.arange(MLA_NOPE), c0 + np.arange(MLA_NOPE)] = 1.0
        pv[h * hkv + MLA_NOPE + np.arange(MLA_V), h * MLA_V + np.arange(MLA_V)] = 1.0
        e1[np.arange(MLA_ROPE), c0 + MLA_NOPE + np.arange(MLA_ROPE)] = 1.0
    place = lambda w, p: jnp.einsum("lij,jk->lik", w, jnp.asarray(p), precision=HIGHEST).astype(bf16)
    return (place(wuq, pqa), place(wuq, pqb), place(wukv, pk), place(wukv, pv),
            jnp.asarray(e1).astype(bf16), jnp.asarray(rot @ e1).astype(bf16))


def _mla_call(p_q, p_kv, cos, sin, qn_g, kvn_g, weights, layer, n_lat, n_q):
    bsz, n, q_rank = p_q.shape
    wqa, wqb, wk, wv, e1, e2 = weights
    hw = MLA_HEADS * HEAD_PAD
    full = lambda shape: pl.BlockSpec(shape, lambda b, i: tuple(0 for _ in shape))
    per_layer = lambda w: pl.BlockSpec((None,) + w.shape[1:], lambda b, i: (layer, 0, 0))
    return pl.pallas_call(
        functools.partial(_mla_kernel, n_lat=n_lat, n_all=n, n_q=n_q),
        grid=(bsz, n_q // ATT_TQ),
        in_specs=[pl.BlockSpec((1, ATT_TQ, q_rank), lambda b, i: (b, i, 0)),
                  pl.BlockSpec((1, n, N_KV_IN), lambda b, i: (b, 0, 0)),
                  full((n, HEAD_PAD)), full((n, HEAD_PAD)),
                  full((1, q_rank)), per_layer(wqa), per_layer(wqb),
                  full((1, MLA_KV_RANK)), per_layer(wk), per_layer(wv),
                  full((MLA_ROPE, hw)), full((MLA_ROPE, hw))],
        out_specs=pl.BlockSpec((1, ATT_TQ, GROUP_W), lambda b, i: (b, i, 0)),
        out_shape=jax.ShapeDtypeStruct((bsz, n_q, GROUP_W), bf16),
        scratch_shapes=[pltpu.VMEM((n, hw), bf16), pltpu.VMEM((MLA_HEADS * MLA_V, n), bf16)],
        compiler_params=_cparams(("parallel", "arbitrary")),
        name="mla",
    )(p_q, p_kv, cos, sin, qn_g.reshape(1, -1), wqa, wqb, kvn_g.reshape(1, -1), wk, wv, e1, e2)


MLP_TF = 1024


def _tail_kernel(yc_ref, yp_ref, ya_ref, y0_ref, y1_ref, bonus_ref, gate_ref, x_ref, m_ref, lg_ref, lb_ref,
                 gavg_ref, g_ref, wo_ref, w1_ref, w2_ref, fg_ref, o_ref, x1_scr, h2_scr, acc_scr,
                 *, n_lat, n_rows, tm, final_norm):
    kf = pl.program_id(2)
    is_ctx = _is_ctx_rows(pl.program_id(1), tm, n_lat, n_rows)

    @pl.when(kf == 0)
    def _():
        o = _group_stats_norm(y0_ref[0] + y1_ref[0], gavg_ref[...], RW_LNX_EPS) * lg_ref[...] + lb_ref[...]
        y_rw = ((o + bonus_ref[0]) * gate_ref[0]).astype(bf16)
        y = jnp.concatenate([yc_ref[0], y_rw, yp_ref[0], ya_ref[0]], axis=1)
        x1 = x_ref[0] + _mod_row(m_ref, 2, is_ctx) * jnp.dot(y, wo_ref[...], preferred_element_type=f32)
        x1_scr[...] = x1
        h2 = _rms(x1, g_ref[...]) * (1.0 + _mod_row(m_ref, 4, is_ctx)) + _mod_row(m_ref, 3, is_ctx)
        h2_scr[...] = h2.astype(bf16)
        acc_scr[...] = jnp.zeros_like(acc_scr)

    z = jnp.maximum(jnp.dot(h2_scr[...], w1_ref[...], preferred_element_type=f32), 0.0)
    acc_scr[...] += jnp.dot((z * z).astype(bf16), w2_ref[...], preferred_element_type=f32)

    @pl.when(kf == pl.num_programs(2) - 1)
    def _():
        x2 = x1_scr[...] + _mod_row(m_ref, 5, is_ctx) * acc_scr[...]
        o_ref[0] = _rms(x2, fg_ref[...]) if final_norm else x2


def _tail_call(y_conv, y_pool, y_att, y0, y1, bonus, gate, xx, modall, lnx_g, lnx_b, g, w_out, w1, w2, final_g,
               layer, n_lat, n_rows, final_norm):
    bsz, _, d = xx.shape
    dff = w1.shape[2]
    tm = next(t for t in (1024, 768, 512, 256) if n_rows % t == 0)
    row = lambda w: pl.BlockSpec((1, tm, w), lambda b, i, k: (b, i, 0))
    vec = lambda w: pl.BlockSpec((1, w), lambda b, i, k: (0, 0))
    return pl.pallas_call(
        functools.partial(_tail_kernel, n_lat=n_lat, n_rows=n_rows, tm=tm, final_norm=final_norm),
        grid=(bsz, n_rows // tm, dff // MLP_TF),
        in_specs=[row(GROUP_W)] * 7 + [row(d),
                  pl.BlockSpec((1, 12, d), lambda b, i, k: (b, 0, 0)),
                  vec(GROUP_W), vec(GROUP_W),
                  pl.BlockSpec((GROUP_W, GROUP_W), lambda b, i, k: (0, 0)),
                  vec(d),
                  pl.BlockSpec((None, d, d), lambda b, i, k: (layer, 0, 0)),
                  pl.BlockSpec((None, d, MLP_TF), lambda b, i, k: (layer, 0, k)),
                  pl.BlockSpec((None, MLP_TF, d), lambda b, i, k: (layer, k, 0)),
                  vec(d)],
        out_specs=row(d),
        out_shape=jax.ShapeDtypeStruct((bsz, n_rows, d), f32),
        scratch_shapes=[pltpu.VMEM((tm, d), f32), pltpu.VMEM((tm, d), bf16), pltpu.VMEM((tm, d), f32)],
        compiler_params=_cparams(("parallel", "parallel", "arbitrary")),
        name="tail",
    )(y_conv, y_pool, y_att, y0, y1, bonus, gate, xx, modall, lnx_g.reshape(1, -1), lnx_b.reshape(1, -1),
      _group_avg_matrix(GROUP_W, RW_HEAD).astype(bf16), g.reshape(1, d), w_out, w1, w2, final_g.reshape(1, d))


def kernel(x, c, ctx, c_ctx, ada_w, ada_b, norm1_g, norm2_g, w_in, w_out, conv_dw, conv_db, conv_gn_g, conv_gn_b, conv_pw, pool_w, pool_scale, rw_mu_prev, rw_mu_next, rw_w0, rw_w2, rw_a0, rw_a2, rw_g2, rw_kk, rw_ka, rw_rk, rw_lnx_g, rw_lnx_b, mla_qn_g, mla_wuq, mla_kvn_g, mla_wukv, mlp_w1, mlp_w2, final_g):
    bsz, n_lat, d = x.shape
    n_ctx = ctx.shape[1]
    depth = ada_w.shape[0]
    n_all = n_ctx + n_lat
    assert n_ctx % max(RW_TILE, ATT_TQ, CONV_ROWS) == 0 and n_lat % max(RW_TILE, ATT_TQ, CONV_ROWS) == 0
    assert n_lat % GRID_W == 0 and bsz + 1 <= 24

    xx = jnp.concatenate([x, ctx], axis=1)
    s_all = jnp.zeros((24, d), f32).at[:bsz].set(c).at[bsz].set(c_ctx)
    cos, sin = _rope_tables(n_ctx, n_lat)
    w_in_pad = jnp.pad(w_in, ((0, 0), (0, 0), (0, (-P_IN) % LANES))).astype(bf16)
    w_out_b, w1_b, w2_b = w_out.astype(bf16), mlp_w1.astype(bf16), mlp_w2.astype(bf16)
    mla_w = _mla_weights(mla_wuq, mla_wukv)

    for l in range(depth):
        last = l == depth - 1
        mod = _ada_call(s_all, ada_w, ada_b, l)
        mod_lat = mod[:bsz].reshape(bsz, 6, d)
        mod_ctx = jnp.broadcast_to(mod[bsz].reshape(1, 6, d), (bsz, 6, d))
        modall = jnp.concatenate([mod_ctx, mod_lat], axis=1)

        n_rows = n_lat if last else n_all
        segs = ((0, n_lat),) if last else ((0, n_lat), (n_lat, n_ctx))
        p_conv, p_pool, p_q, p_rw, p_kv = _inproj_call(xx, modall, norm1_g[l], w_in_pad, l, n_lat)
        y_conv = _conv_call(p_conv, conv_dw[l], conv_db[l], conv_gn_g[l], conv_gn_b[l], conv_pw[l], segs)
        y_pool = _pool_call(p_pool, pool_w[l], pool_scale[l], segs)
        prep = _rwprep_call(p_rw, rw_mu_prev[l], rw_mu_next[l], rw_w0[l], rw_w2[l], rw_a0[l], rw_a2[l],
                            rw_g2[l], rw_kk[l], rw_ka[l], rw_rk[l], n_lat)
        y0, y1 = _rwscan_call(prep, n_ctx)
        y_att = _mla_call(p_q, p_kv, cos, sin, mla_qn_g[l], mla_kvn_g[l], mla_w, l, n_lat, n_rows)
        xx = _tail_call(y_conv, y_pool, y_att, y0, y1, prep[17], prep[18], xx, modall, rw_lnx_g[l], rw_lnx_b[l],
                        norm2_g[l], w_out_b, w1_b, w2_b, final_g, l, n_lat, n_rows, last)
    return xx
```

```python
import functools

import numpy as np
import jax
import jax.numpy as jnp
from jax import lax
from jax.experimental import pallas as pl
from jax.experimental.pallas import tpu as pltpu

f32 = jnp.float32
bf16 = jnp.bfloat16
HIGHEST = lax.Precision.HIGHEST

GROUP_W = 256
NORM_EPS = 1e-6
GN_EPS = 1e-5
CONV_K = 31
CONV_NORM_GROUPS = 4
RW_HEAD = 64
RW_HEADS = GROUP_W // RW_HEAD
RW_DECAY_LORA = 64
RW_A_LORA = 64
RW_GATE_LORA = 128
RW_LNX_EPS = 64e-5
POOL_WINDOWS = (2, 4, 8, 16)
MLA_HEADS = 4
MLA_NOPE = 64
MLA_ROPE = 32
MLA_V = 64
MLA_KV_RANK = 128
ROPE_BASE = 10000.0
GRID_W = 64
N_CONV_IN = 2 * GROUP_W
N_RW_IN = 3 * GROUP_W + RW_GATE_LORA + 2 * RW_DECAY_LORA + 2 * RW_A_LORA
N_KV_IN = MLA_KV_RANK + MLA_ROPE
OFF_POOL = N_CONV_IN
OFF_Q = OFF_POOL + GROUP_W
OFF_RW = OFF_Q + GROUP_W
OFF_KV = OFF_RW + N_RW_IN
P_IN = OFF_KV + N_KV_IN

LANES = 128
SUBLANES = 8
VMEM_LIMIT_BYTES = 56 * 1024 * 1024

RW_CHUNK = 64
HEAD_PAD = 128


def _cparams(sem):
    return pltpu.CompilerParams(dimension_semantics=sem, vmem_limit_bytes=VMEM_LIMIT_BYTES)


def _dot(a, b):
    return jnp.dot(a.astype(bf16), b.astype(bf16), preferred_element_type=f32)


def _dot_hi(a, b):
    return jnp.dot(a, b, preferred_element_type=f32, precision=HIGHEST)


def _dot_nt(a, b):
    return lax.dot_general(a.astype(bf16), b.astype(bf16), (((1,), (1,)), ((), ())), preferred_element_type=f32)


def _dot_tn(a, b):
    return lax.dot_general(a.astype(bf16), b.astype(bf16), (((0,), (0,)), ((), ())), preferred_element_type=f32)


def _dot_split(a, b, terms):
    split_lhs = a.dtype == f32
    x = a if split_lhs else b
    acc = None
    for _ in range(terms):
        piece = x.astype(bf16)
        part = jnp.dot(piece if split_lhs else a, b if split_lhs else piece, preferred_element_type=f32)
        acc = part if acc is None else acc + part
        x = x - piece.astype(f32)
    return acc


def _dot_3pass(a, b):
    a_hi, b_hi = a.astype(bf16), b.astype(bf16)
    a_lo, b_lo = (a - a_hi.astype(f32)).astype(bf16), (b - b_hi.astype(f32)).astype(bf16)
    return (jnp.dot(a_hi, b_hi, preferred_element_type=f32) + jnp.dot(a_lo, b_hi, preferred_element_type=f32)
            + jnp.dot(a_hi, b_lo, preferred_element_type=f32))


def _sigmoid(x):
    return jax.nn.sigmoid(x)


def _mod_row(m_ref, j, is_ctx):
    lat = m_ref[0, 6 + j:7 + j, :]
    if is_ctx is None:
        return lat
    return jnp.where(is_ctx, m_ref[0, j:j + 1, :], lat)


def _is_ctx_rows(tile_idx, tm, n_lat, n_rows):
    if n_rows <= n_lat:
        return None
    row = tile_idx * tm + lax.broadcasted_iota(jnp.int32, (tm, 1), 0)
    return row >= n_lat


def _group_stats_norm(y, gavg, eps):
    dlt = y - _dot_split(y, gavg, 2)
    return dlt * lax.rsqrt(_dot_split(dlt * dlt, gavg, 2) + eps)


def _rms(x, g):
    return x * lax.rsqrt(jnp.mean(x * x, axis=-1, keepdims=True) + NORM_EPS) * g


def _ada_kernel(s_ref, w_ref, b_ref, o_ref):
    s = s_ref[...]
    s = s * _sigmoid(s)
    o_ref[...] = _dot_hi(s, w_ref[...]) + b_ref[...]


def _ada_call(s_all, w, b, layer):
    rows, d = s_all.shape
    depth, _, n = w.shape
    tn = 1536
    return pl.pallas_call(
        _ada_kernel,
        grid=(n // tn,),
        in_specs=[pl.BlockSpec((rows, d), lambda j: (0, 0)),
                  pl.BlockSpec((None, d, tn), lambda j: (layer, 0, j)),
                  pl.BlockSpec((None, 1, tn), lambda j: (layer, 0, j))],
        out_specs=pl.BlockSpec((rows, tn), lambda j: (0, j)),
        out_shape=jax.ShapeDtypeStruct((rows, n), f32),
        compiler_params=_cparams(("arbitrary",)),
        name="ada",
    )(s_all, w, b.reshape(depth, 1, n))


def _inproj_kernel(x_ref, m_ref, g_ref, w_ref, oc_ref, op_ref, oq_ref, orw_ref, okv_ref, *, n_lat, n_rows, tm):
    is_ctx = _is_ctx_rows(pl.program_id(1), tm, n_lat, n_rows)
    h = _rms(x_ref[0], g_ref[...]) * (1.0 + _mod_row(m_ref, 1, is_ctx)) + _mod_row(m_ref, 0, is_ctx)
    p = jnp.dot(h.astype(bf16), w_ref[...], preferred_element_type=f32)
    oc_ref[0] = p[:, 0:OFF_POOL]
    op_ref[0] = p[:, OFF_POOL:OFF_Q]
    oq_ref[0] = p[:, OFF_Q:OFF_RW]
    orw_ref[0] = p[:, OFF_RW:OFF_KV]
    okv_ref[0] = p[:, OFF_KV:P_IN]


def _inproj_call(xx, modall, g, w_pad, layer, n_lat):
    bsz, n, d = xx.shape
    tm = next(t for t in (768, 512, 384, 256) if n % t == 0)
    widths = (N_CONV_IN, GROUP_W, GROUP_W, N_RW_IN, N_KV_IN)
    return pl.pallas_call(
        functools.partial(_inproj_kernel, n_lat=n_lat, n_rows=n, tm=tm),
        grid=(bsz, n // tm),
        in_specs=[pl.BlockSpec((1, tm, d), lambda b, i: (b, i, 0)),
                  pl.BlockSpec((1, 12, d), lambda b, i: (b, 0, 0)),
                  pl.BlockSpec((1, d), lambda b, i: (0, 0)),
                  pl.BlockSpec((None,) + w_pad.shape[1:], lambda b, i: (layer, 0, 0))],
        out_specs=[pl.BlockSpec((1, tm, w), lambda b, i: (b, i, 0)) for w in widths],
        out_shape=[jax.ShapeDtypeStruct((bsz, n, w), f32) for w in widths],
        compiler_params=_cparams(("parallel", "parallel")),
        name="inproj",
    )(xx, modall, g.reshape(1, d), w_pad)


CONV_ROWS = 128
CONV_HALO = 16
CONV_NORM_ROWS = 256
CONV_NORM_GROUPS_PER_STEP = 4


def _conv_kernel(p_ref, dw_ref, db_ref, gg_ref, gb_ref, pw_ref, gavg_ref, o_ref, u_scr, y_scr, *, segs):
    r, hl = CONV_ROWS, CONV_HALO
    win_rows = r + 2 * hl
    for s0, n in segs:
        u_scr[0:hl, :] = jnp.zeros((hl, GROUP_W), f32)
        u_scr[hl + n:hl + n + hl, :] = jnp.zeros((hl, GROUP_W), f32)

        def fill(c, carry, s0=s0):
            r0 = pl.multiple_of(c * r, r)
            blk = p_ref[0, pl.ds(s0 + r0, r), :]
            u_scr[pl.ds(hl + r0, r), :] = blk[:, :GROUP_W] * _sigmoid(blk[:, GROUP_W:])
            return carry

        lax.fori_loop(0, n // r, fill, 0)

        def taps(c, carry):
            r0 = pl.multiple_of(c * r, r)
            win = u_scr[pl.ds(r0, win_rows), :]
            rolled = [win] + [pltpu.roll(win, win_rows - b, axis=0) for b in range(1, SUBLANES)]
            acc = jnp.zeros((r, GROUP_W), f32)
            for j in range(CONV_K):
                off = hl - CONV_K // 2 + j
                base = off - off % SUBLANES
                acc = acc + rolled[off % SUBLANES][base:base + r] * dw_ref[j:j + 1, :]
            y_scr[pl.ds(r0, r), :] = acc + db_ref[...]
            return carry

        lax.fori_loop(0, n // r, taps, 0)

        gr = CONV_NORM_ROWS
        per_step = CONV_NORM_GROUPS_PER_STEP if n % (gr * CONV_NORM_GROUPS_PER_STEP) == 0 else 1

        def norm_project(c, carry, s0=s0, per_step=per_step):
            starts = [pl.multiple_of((c * per_step + g) * gr, gr) for g in range(per_step)]
            ys = [y_scr[pl.ds(t0, gr), :] for t0 in starts]
            dl = [y - _dot_split(y, gavg_ref[...], 2) for y in ys]
            var = [_dot_split(d * d, gavg_ref[...], 2) for d in dl]
            yn = [d * lax.rsqrt(v + GN_EPS) * gg_ref[...] + gb_ref[...] for d, v in zip(dl, var)]
            out = [_dot(t * _sigmoid(t), pw_ref[...]).astype(bf16) for t in yn]
            for t0, o in zip(starts, out):
                o_ref[0, pl.ds(s0 + t0, gr), :] = o
            return carry

        lax.fori_loop(0, n // (gr * per_step), norm_project, 0)


def _group_avg_matrix(width, group):
    idx = np.arange(width) // group
    return jnp.asarray((idx[:, None] == idx[None, :]).astype(np.float32) / group)


def _conv_call(p_conv, dw, db, gg, gb, pw, segs):
    bsz, n, _ = p_conv.shape
    n_out = max(s[0] + s[1] for s in segs)
    gavg = _group_avg_matrix(GROUP_W, GROUP_W // CONV_NORM_GROUPS).astype(bf16)
    max_seg = max(s[1] for s in segs)
    vec = lambda: pl.BlockSpec((1, GROUP_W), lambda b: (0, 0))
    return pl.pallas_call(
        functools.partial(_conv_kernel, segs=segs),
        grid=(bsz,),
        in_specs=[pl.BlockSpec((1, n, N_CONV_IN), lambda b: (b, 0, 0)),
                  pl.BlockSpec((CONV_K, GROUP_W), lambda b: (0, 0)),
                  vec(), vec(), vec(),
                  pl.BlockSpec((GROUP_W, GROUP_W), lambda b: (0, 0)),
                  pl.BlockSpec((GROUP_W, GROUP_W), lambda b: (0, 0))],
        out_specs=pl.BlockSpec((1, n_out, GROUP_W), lambda b: (b, 0, 0)),
        out_shape=jax.ShapeDtypeStruct((bsz, n_out, GROUP_W), bf16),
        scratch_shapes=[pltpu.VMEM((max_seg + 2 * CONV_HALO, GROUP_W), f32),
                        pltpu.VMEM((max_seg, GROUP_W), f32)],
        compiler_params=_cparams(("parallel",)),
        name="conv",
    )(p_conv, dw, db.reshape(1, -1), gg.reshape(1, -1), gb.reshape(1, -1), pw.astype(bf16), gavg)


def _pool_kernel(p_ref, w_ref, sc_ref, o_ref, u_scr, *, segs):
    r, hl = CONV_ROWS, CONV_HALO
    win_rows = r + 2 * hl
    pool_ch = GROUP_W // len(POOL_WINDOWS)
    lane = lax.broadcasted_iota(jnp.int32, (1, GROUP_W), 1)
    half = jnp.full((1, GROUP_W), POOL_WINDOWS[-1] // 2, jnp.int32)
    for gi in range(len(POOL_WINDOWS) - 2, -1, -1):
        half = jnp.where(lane < (gi + 1) * pool_ch, POOL_WINDOWS[gi] // 2, half)

    def shifted(v, k):
        return pltpu.roll(v, (win_rows - k) % win_rows, axis=0)

    for s0, n in segs:
        u_scr[0:hl, :] = jnp.zeros((hl, GROUP_W), f32)
        u_scr[hl + n:hl + n + hl, :] = jnp.zeros((hl, GROUP_W), f32)

        def fill(c, carry, s0=s0):
            r0 = pl.multiple_of(c * r, r)
            u_scr[pl.ds(hl + r0, r), :] = p_ref[0, pl.ds(s0 + r0, r), :]
            return carry

        lax.fori_loop(0, n // r, fill, 0)

        def body(c, carry, s0=s0, n=n):
            r0 = pl.multiple_of(c * r, r)
            win = u_scr[pl.ds(r0, win_rows), :]
            s2 = win + shifted(win, -1)
            s4 = shifted(s2, -1) + shifted(s2, 1)
            s8 = shifted(s4, -2) + shifted(s4, 2)
            s16 = shifted(s8, -4) + shifted(s8, 4)
            sums = (s2, s4, s8, s16)
            sel = sums[-1]
            for gi in range(len(POOL_WINDOWS) - 2, -1, -1):
                sel = jnp.where(lane < (gi + 1) * pool_ch, sums[gi], sel)
            sel = sel[hl:hl + r]
            u = win[hl:hl + r]
            t = r0 + lax.broadcasted_iota(jnp.int32, (r, 1), 0)
            cnt = (jnp.minimum(t + half, n) - jnp.maximum(t - half, 0)).astype(f32)
            dlt = sel / cnt - u
            o_ref[0, pl.ds(s0 + r0, r), :] = (_dot(dlt, w_ref[...]) * sc_ref[...]).astype(bf16)
            return carry

        lax.fori_loop(0, n // r, body, 0)


def _block_diag(blocks):
    g, a, b = blocks.shape
    out = jnp.zeros((g * a, g * b), blocks.dtype)
    for i in range(g):
        out = out.at[i * a:(i + 1) * a, i * b:(i + 1) * b].set(blocks[i])
    return out


def _pool_call(p_pool, pool_w, pool_scale, segs):
    bsz, n, _ = p_pool.shape
    n_out = max(s[0] + s[1] for s in segs)
    max_seg = max(s[1] for s in segs)
    return pl.pallas_call(
        functools.partial(_pool_kernel, segs=segs),
        grid=(bsz,),
        in_specs=[pl.BlockSpec((1, n, GROUP_W), lambda b: (b, 0, 0)),
                  pl.BlockSpec((GROUP_W, GROUP_W), lambda b: (0, 0)),
                  pl.BlockSpec((1, GROUP_W), lambda b: (0, 0))],
        out_specs=pl.BlockSpec((1, n_out, GROUP_W), lambda b: (b, 0, 0)),
        out_shape=jax.ShapeDtypeStruct((bsz, n_out, GROUP_W), bf16),
        scratch_shapes=[pltpu.VMEM((max_seg + 2 * CONV_HALO, GROUP_W), f32)],
        compiler_params=_cparams(("parallel",)),
        name="pool",
    )(p_pool, _block_diag(pool_w).astype(bf16), pool_scale.reshape(1, -1))


RW_TILE = 256
RW_R, RW_K, RW_V, RW_G = 0, GROUP_W, 2 * GROUP_W, 3 * GROUP_W
RW_W = RW_G + RW_GATE_LORA
RW_A = RW_W + 2 * RW_DECAY_LORA
RW_STACK = RW_HEADS * RW_CHUNK


def _stack_heads(x, hm):
    return jnp.concatenate([x] * RW_HEADS, axis=0) * hm


def _unstack_heads(z):
    c = z.shape[0] // RW_HEADS
    out = z[0:c]
    for h in range(1, RW_HEADS):
        out = out + z[h * c:(h + 1) * c]
    return out


def _scan_order_masks():
    ri = lax.broadcasted_iota(jnp.int32, (RW_STACK, RW_STACK), 0)
    ci = lax.broadcasted_iota(jnp.int32, (RW_STACK, RW_STACK), 1)
    return ((ri > ci, ri >= ci), (ri < ci, ri <= ci)), (ri == ci).astype(f32)


def _head_mask():
    m = np.arange(RW_STACK)[:, None] // RW_CHUNK == np.arange(GROUP_W)[None, :] // RW_HEAD
    return jnp.asarray(m.astype(np.float32))


def _rwprep_kernel(p_ref, hp_ref, hn_ref, mup_ref, mun_ref, w0_ref, w2_ref, a0_ref, a2_ref, g2_ref,
                   kk_ref, ka_ref, rk_ref, tril_ref, triu_ref, hsum_ref, hm_ref,
                   rt0, bt0, bp0, kp0, kh0, wc0, yp0, pc0, rt1, bt1, bp1, kp1, kh1, wc1, yp1, pc1,
                   v_out, bonus_out, gate_out, *, seg_starts, seg_ends):
    tr, c = RW_TILE, RW_CHUNK
    i = pl.program_id(1)
    row0 = i * tr
    first = functools.reduce(jnp.logical_or, [row0 == s for s in seg_starts])
    last = functools.reduce(jnp.logical_or, [row0 + tr == e for e in seg_ends])
    ridx = lax.broadcasted_iota(jnp.int32, (tr, 1), 0)

    def zcols(a, b):
        p = p_ref[0, :, a:b]
        prev_row = jnp.where(first, 0.0, hp_ref[0, SUBLANES - 1:SUBLANES, a:b])
        next_row = jnp.where(last, 0.0, hn_ref[0, 0:1, a:b])
        prev = jnp.where(ridx == 0, prev_row, pltpu.roll(p, 1, axis=0))
        nxt = jnp.where(ridx == tr - 1, next_row, pltpu.roll(p, tr - 1, axis=0))
        return p + mup_ref[:, a:b] * (prev - p) + mun_ref[:, a:b] * (nxt - p)

    r = zcols(RW_R, RW_K)
    k = zcols(RW_K, RW_V)
    v = zcols(RW_V, RW_G)
    vb = v.astype(bf16)
    v_out[0] = vb
    gate_out[0] = _dot(_sigmoid(zcols(RW_G, RW_W)), g2_ref[...])
    kk = k * kk_ref[...]
    kk = kk / jnp.maximum(jnp.sqrt(_dot_split(kk * kk, hsum_ref[...], 2)), 1e-12)
    w_all = _dot_3pass(jnp.tanh(zcols(RW_W, RW_A)), w2_ref[...]) + w0_ref[...]
    a_all = _sigmoid(_dot_3pass(zcols(RW_A, N_RW_IN), a2_ref[...]) + a0_ref[...])
    kd_sum = jnp.zeros_like(k)
    outs = ((rt0, bt0, bp0, kp0, kh0, wc0, yp0, pc0, tril_ref), (rt1, bt1, bp1, kp1, kh1, wc1, yp1, pc1, triu_ref))
    scaled = []
    for d, (rt_o, bt_o, bp_o, kp_o, _, _, _, pc_o, tri_ref) in enumerate(outs):
        x = w_all[:, d * GROUP_W:(d + 1) * GROUP_W]
        neg = -x
        log_w = -(jnp.maximum(neg, 0.0) + jnp.log1p(jnp.exp(-jnp.abs(neg)))) - 0.5
        lw = -jnp.exp(log_w)
        a = a_all[:, d * GROUP_W:(d + 1) * GROUP_W]
        kd = k * (1.0 + (a - 1.0) * ka_ref[...])
        kd_sum = kd_sum + kd
        b = kk * a
        cum = _dot_split(tri_ref[...], lw, 3)
        tot_rows = []
        for ci in range(tr // c):
            edge = ci * c + (c - 1 if d == 0 else 0)
            tot_rows.append(cum[edge:edge + 1, :])
            pc_o[0, ci * SUBLANES:(ci + 1) * SUBLANES, :] = jnp.broadcast_to(jnp.exp(tot_rows[-1]), (SUBLANES, GROUP_W))
        tot = jnp.concatenate([jnp.broadcast_to(t, (c, GROUP_W)) for t in tot_rows], axis=0)
        e_neg = jnp.exp(-cum)
        e_rest = jnp.exp(tot - cum)
        rt = (r * jnp.exp(cum)).astype(bf16)
        kq = (kk * jnp.exp(cum - lw)).astype(bf16)
        bt = (b * e_neg).astype(bf16)
        kt = (kd * e_neg).astype(bf16)
        rt_o[0] = rt
        bt_o[0] = bt
        bp_o[0] = (b * e_rest).astype(bf16)
        kp_o[0] = (kd * e_rest).astype(bf16)
        scaled.append((rt, kq, bt, kt))
    bonus_out[0] = _dot_split(r * rk_ref[...] * kd_sum, hsum_ref[...], 2) * v

    hm = hm_ref[...]
    masks, eye = _scan_order_masks()
    chains = [(d, ci) for d in range(2) for ci in range(tr // c)]
    rows = lambda ci: slice(ci * c, (ci + 1) * c)
    rs = [_stack_heads(scaled[d][0][rows(ci)], hm) for d, ci in chains]
    ks = [_stack_heads(scaled[d][1][rows(ci)], hm) for d, ci in chains]
    bs = [_stack_heads(scaled[d][2][rows(ci)], hm) for d, ci in chains]
    kts = [_stack_heads(scaled[d][3][rows(ci)], hm) for d, ci in chains]
    vs = [_stack_heads(vb[rows(ci)], hm) for d, ci in chains]
    a_ub = [jnp.where(masks[d][0], _dot_nt(ks[q], bs[q]), 0.0) for q, (d, ci) in enumerate(chains)]
    a_vk = [jnp.where(masks[d][0], _dot_nt(ks[q], kts[q]), 0.0) for q, (d, ci) in enumerate(chains)]
    a_rk = [jnp.where(masks[d][1], _dot_nt(rs[q], kts[q]), 0.0) for q, (d, ci) in enumerate(chains)]
    tinv = [eye - a for a in a_ub]
    apow = a_ub
    for _ in range(int(np.log2(c)) - 1):
        apow = [_dot(a, a) for a in apow]
        tinv = [t + _dot(t, a) for t, a in zip(tinv, apow)]
    for q, (d, ci) in enumerate(chains):
        kh_o, wc_o, yp_o = outs[d][4], outs[d][5], outs[d][6]
        kh_o[0, rows(ci), :] = _unstack_heads(_dot(tinv[q], ks[q])).astype(bf16)
        wc_o[0, rows(ci), :] = _unstack_heads(_dot(tinv[q], _dot(a_vk[q], vs[q])))
        yp_o[0, rows(ci), :] = _unstack_heads(_dot(a_rk[q], vs[q]))


def _chunk_tri(tile, chunk, upper):
    t = np.arange(tile)
    same = (t[:, None] // chunk) == (t[None, :] // chunk)
    tri = (t[None, :] >= t[:, None]) if upper else (t[None, :] <= t[:, None])
    return jnp.asarray((same & tri).astype(np.float32))


def _rwprep_call(p_rw, mu_prev, mu_next, w0, w2, a0, a2, g2, kkp, kap, rk, n_lat):
    bsz, n, _ = p_rw.shape
    tr = RW_TILE
    nb8 = n // SUBLANES
    seg_starts = tuple(sorted({0, n_lat} - {n}))
    seg_ends = tuple(sorted({n_lat, n}))
    zeros = jnp.zeros((RW_DECAY_LORA, GROUP_W), f32)
    w2cat = jnp.concatenate([jnp.concatenate([w2[0], zeros], 1), jnp.concatenate([zeros, w2[1]], 1)], 0)
    a2cat = jnp.concatenate([jnp.concatenate([a2[0], zeros], 1), jnp.concatenate([zeros, a2[1]], 1)], 0)
    hsum = _group_avg_matrix(GROUP_W, RW_HEAD) * RW_HEAD
    full = lambda shape: pl.BlockSpec(shape, lambda b, i: tuple(0 for _ in shape))
    row_spec = pl.BlockSpec((1, tr, GROUP_W), lambda b, i: (b, i, 0))
    pc_rows = tr // RW_CHUNK * SUBLANES
    pc_spec = pl.BlockSpec((1, pc_rows, GROUP_W), lambda b, i: (b, i, 0))
    arr = lambda dt: jax.ShapeDtypeStruct((bsz, n, GROUP_W), dt)
    pc_arr = jax.ShapeDtypeStruct((bsz, n // RW_CHUNK * SUBLANES, GROUP_W), f32)
    dir_specs = [row_spec] * 7 + [pc_spec]
    dir_shapes = [arr(bf16)] * 5 + [arr(f32), arr(f32), pc_arr]
    return pl.pallas_call(
        functools.partial(_rwprep_kernel, seg_starts=seg_starts, seg_ends=seg_ends),
        grid=(bsz, n // tr),
        in_specs=[pl.BlockSpec((1, tr, N_RW_IN), lambda b, i: (b, i, 0)),
                  pl.BlockSpec((1, SUBLANES, N_RW_IN),
                               lambda b, i: (b, jnp.maximum(i * (tr // SUBLANES) - 1, 0), 0)),
                  pl.BlockSpec((1, SUBLANES, N_RW_IN),
                               lambda b, i: (b, jnp.minimum((i + 1) * (tr // SUBLANES), nb8 - 1), 0)),
                  full((1, N_RW_IN)), full((1, N_RW_IN)),
                  full((1, 2 * GROUP_W)), full((2 * RW_DECAY_LORA, 2 * GROUP_W)),
                  full((1, 2 * GROUP_W)), full((2 * RW_A_LORA, 2 * GROUP_W)),
                  full((RW_GATE_LORA, GROUP_W)),
                  full((1, GROUP_W)), full((1, GROUP_W)), full((1, GROUP_W)),
                  full((tr, tr)), full((tr, tr)), full((GROUP_W, GROUP_W)), full((RW_STACK, GROUP_W))],
        out_specs=dir_specs * 2 + [row_spec] * 3,
        out_shape=dir_shapes * 2 + [arr(bf16), arr(f32), arr(f32)],
        compiler_params=_cparams(("parallel", "parallel")),
        name="rwprep",
    )(p_rw, p_rw, p_rw, mu_prev.reshape(1, -1), mu_next.reshape(1, -1),
      w0.reshape(1, -1), w2cat, a0.reshape(1, -1), a2cat, g2.astype(bf16),
      kkp.reshape(1, -1), kap.reshape(1, -1), rk.reshape(1, -1),
      _chunk_tri(tr, RW_CHUNK, False).astype(bf16), _chunk_tri(tr, RW_CHUNK, True).astype(bf16),
      hsum.astype(bf16), _head_mask().astype(bf16))


RW_SCAN_BATCH = 8


def _rwscan_kernel(*refs, nb):
    nd = 9
    dir_refs = (refs[0:nd], refs[nd:2 * nd])
    hm_ref, y_refs, s_scr = refs[2 * nd], refs[2 * nd + 1:2 * nd + 3], refs[2 * nd + 3]

    @pl.when(pl.program_id(1) == 0)
    def _():
        s_scr[...] = jnp.zeros_like(s_scr)

    hm = hm_ref[...]
    hm32 = hm.astype(f32)
    masks, _ = _scan_order_masks()
    chains = [(d, bb) for bb in range(nb) for d in range(2)]
    ld = lambda d, bb, k: dir_refs[d][k][bb]
    rs = [_stack_heads(ld(d, bb, 0), hm) for d, bb in chains]
    bs = [_stack_heads(ld(d, bb, 1), hm) for d, bb in chains]
    bps = [_stack_heads(ld(d, bb, 2), hm) for d, bb in chains]
    kps = [_stack_heads(ld(d, bb, 3), hm) for d, bb in chains]
    khs = [_stack_heads(ld(d, bb, 4), hm) for d, bb in chains]
    w2t = [_stack_heads(ld(d, bb, 5), hm32).T for d, bb in chains]
    vs = [_stack_heads(ld(d, bb, 8), hm) for d, bb in chains]
    s = [s_scr[d, bb] for d, bb in chains]
    sb = [x.astype(bf16) for x in s]
    us_t = [-(_dot_nt(sb[j], khs[j]) + w2t[j]) for j in range(len(chains))]
    a_rb = [jnp.where(masks[d][1], _dot_nt(rs[j], bs[j]), 0.0) for j, (d, bb) in enumerate(chains)]
    ds = [_dot(us_t[j], bps[j]) + _dot_tn(vs[j], kps[j]) for j in range(len(chains))]
    for j, (d, bb) in enumerate(chains):
        pc_rows = jnp.concatenate([ld(d, bb, 7)] * (GROUP_W // SUBLANES), axis=0)
        s_scr[d, bb] = s[j] * pc_rows + ds[j]
    ys = [_dot_nt(rs[j], sb[j]) + _dot(a_rb[j], us_t[j].T) for j in range(len(chains))]
    for j, (d, bb) in enumerate(chains):
        y_refs[d][bb] = _unstack_heads(ys[j]) + ld(d, bb, 6)


def _rwscan_call(prep, n_ctx):
    v = prep[16]
    bsz, n, _ = v.shape
    c = RW_CHUNK
    nb = max(t for t in range(1, RW_SCAN_BATCH + 1) if bsz % t == 0)
    n_chunks = n // c
    ctx_chunks = n_ctx // c
    lat_chunks = n_chunks - ctx_chunks

    def fwd(b, i):
        return (b, jnp.where(i < ctx_chunks, lat_chunks + i, i - ctx_chunks), 0)

    def bwd(b, i):
        return (b, n_chunks - 1 - i, 0)

    blk = (nb, c, GROUP_W)
    pcb = (nb, SUBLANES, GROUP_W)
    dir_specs = lambda im: [pl.BlockSpec(blk, im)] * 7 + [pl.BlockSpec(pcb, im), pl.BlockSpec(blk, im)]
    return pl.pallas_call(
        functools.partial(_rwscan_kernel, nb=nb),
        grid=(bsz // nb, n_chunks),
        in_specs=dir_specs(fwd) + dir_specs(bwd) + [pl.BlockSpec((RW_STACK, GROUP_W), lambda b, i: (0, 0))],
        out_specs=[pl.BlockSpec(blk, fwd), pl.BlockSpec(blk, bwd)],
        out_shape=[jax.ShapeDtypeStruct((bsz, n, GROUP_W), f32)] * 2,
        scratch_shapes=[pltpu.VMEM((2, nb, GROUP_W, GROUP_W), f32)],
        compiler_params=_cparams(("parallel", "arbitrary")),
        name="rwscan",
    )(*prep[0:8], v, *prep[8:16], v, _head_mask().astype(bf16))


ATT_TQ = 256
ATT_KV_ROWS = 256


def _mla_kernel(pq_ref, pkv_ref, cos_ref, sin_ref, qg_ref, wqa_ref, wqb_ref, kvg_ref, wk_ref, wv_ref,
                e1_ref, e2_ref, o_ref, k_scr, vt_scr, *, n_lat, n_all, n_q):
    qi = pl.program_id(1)
    tq = ATT_TQ
    hp = HEAD_PAD

    def tile4(t):
        return jnp.concatenate([t] * MLA_HEADS, axis=1)

    @pl.when(qi == 0)
    def _():
        for r0 in range(0, n_all, ATT_KV_ROWS):
            pkv = pkv_ref[0, r0:r0 + ATT_KV_ROWS, :]
            ckv = _rms(pkv[:, :MLA_KV_RANK], kvg_ref[...])
            kr = pkv[:, MLA_KV_RANK:]
            cos = tile4(cos_ref[r0:r0 + ATT_KV_ROWS, :])
            sin = tile4(sin_ref[r0:r0 + ATT_KV_ROWS, :])
            kmat = _dot(ckv, wk_ref[...]) + _dot(kr, e1_ref[...]) * cos + _dot(kr, e2_ref[...]) * sin
            k_scr[r0:r0 + ATT_KV_ROWS, :] = kmat.astype(bf16)
            vt_scr[:, r0:r0 + ATT_KV_ROWS] = _dot(ckv, wv_ref[...]).T.astype(bf16)

    r0 = pl.multiple_of(qi * tq, tq)
    qn = _rms(pq_ref[0], qg_ref[...]).astype(bf16)
    cos = tile4(cos_ref[pl.ds(r0, tq), :])
    sin = tile4(sin_ref[pl.ds(r0, tq), :])
    scale = float(MLA_NOPE + MLA_ROPE) ** -0.5
    q = (jnp.dot(qn, wqa_ref[...], preferred_element_type=f32) * cos
         + jnp.dot(qn, wqb_ref[...], preferred_element_type=f32) * sin) * scale
    q = q.astype(bf16)

    def attend(k0, k1):
        heads = range(MLA_HEADS)
        st = [_dot_nt(k_scr[k0:k1, h * hp:(h + 1) * hp], q[:, h * hp:(h + 1) * hp]) for h in heads]
        e = [jnp.exp(st[h] - jnp.max(st[h], axis=0, keepdims=True)) for h in heads]
        inv = [1.0 / jnp.sum(e[h], axis=0, keepdims=True) for h in heads]
        outs = [_dot(vt_scr[h * MLA_V:(h + 1) * MLA_V, k0:k1], e[h]) * inv[h] for h in heads]
        o_ref[0] = jnp.concatenate(outs, axis=0).T.astype(bf16)

    if n_q > n_lat:
        @pl.when(qi < n_lat // tq)
        def _():
            attend(0, n_all)

        @pl.when(qi >= n_lat // tq)
        def _():
            attend(n_lat, n_all)
    else:
        attend(0, n_all)


def _rope_rotation():
    quarter = MLA_ROPE // 4
    rot = np.zeros((MLA_ROPE, MLA_ROPE), np.float32)
    for axis in range(2):
        for f in range(quarter):
            first, second = axis * 2 * quarter + f, axis * 2 * quarter + quarter + f
            rot[second, first] = -1.0
            rot[first, second] = 1.0
    return rot


def _rope_tables(n_ctx, n_lat):
    rows = n_lat // GRID_W
    row = jnp.repeat(jnp.arange(rows), GRID_W).astype(f32)
    col = jnp.tile(jnp.arange(GRID_W), rows).astype(f32)
    n_freq = MLA_ROPE // 4
    freq = ROPE_BASE ** (-jnp.arange(n_freq, dtype=f32) / n_freq)
    ang = jnp.concatenate([row[:, None] * freq, row[:, None] * freq, col[:, None] * freq, col[:, None] * freq], 1)
    cos = jnp.ones((n_lat + n_ctx, HEAD_PAD), f32).at[:n_lat, MLA_NOPE:MLA_NOPE + MLA_ROPE].set(jnp.cos(ang))
    sin = jnp.zeros((n_lat + n_ctx, HEAD_PAD), f32).at[:n_lat, MLA_NOPE:MLA_NOPE + MLA_ROPE].set(jnp.sin(ang))
    return cos, sin


def _mla_weights(wuq, wukv):
    rot = _rope_rotation()
    hq = MLA_NOPE + MLA_ROPE
    hkv = MLA_NOPE + MLA_V
    hw = MLA_HEADS * HEAD_PAD
    pqa = np.zeros((MLA_HEADS * hq, hw), np.float32)
    pqb = np.zeros_like(pqa)
    pk = np.zeros((MLA_HEADS * hkv, hw), np.float32)
    pv = np.zeros((MLA_HEADS * hkv, MLA_HEADS * MLA_V), np.float32)
    e1 = np.zeros((MLA_ROPE, hw), np.float32)
    for h in range(MLA_HEADS):
        c0 = h * HEAD_PAD
        pqa[h * hq + np.arange(hq), c0 + np.arange(hq)] = 1.0
        pqb[h * hq + MLA_NOPE:(h + 1) * hq, c0 + MLA_NOPE:c0 + hq] = rot
        pk[h * hkv + np.arange(MLA_NOPE), c0 + np.arange(MLA_NOPE)] = 1.0
        pv[h * hkv + MLA_NOPE + np.arange(MLA_V), h * MLA_V + np.arange(MLA_V)] = 1.0
        e1[np.arange(MLA_ROPE), c0 + MLA_NOPE + np.arange(MLA_ROPE)] = 1.0
    place = lambda w, p: jnp.einsum("lij,jk->lik", w, jnp.asarray(p), precision=HIGHEST).astype(bf16)
    return (place(wuq, pqa), place(wuq, pqb), place(wukv, pk), place(wukv, pv),
            jnp.asarray(e1).astype(bf16), jnp.asarray(rot @ e1).astype(bf16))


def _mla_call(p_q, p_kv, cos, sin, qn_g, kvn_g, weights, layer, n_lat, n_q):
    bsz, n, q_rank = p_q.shape
    wqa, wqb, wk, wv, e1, e2 = weights
    hw = MLA_HEADS * HEAD_PAD
    full = lambda shape: pl.BlockSpec(shape, lambda b, i: tuple(0 for _ in shape))
    per_layer = lambda w: pl.BlockSpec((None,) + w.shape[1:], lambda b, i: (layer, 0, 0))
    return pl.pallas_call(
        functools.partial(_mla_kernel, n_lat=n_lat, n_all=n, n_q=n_q),
        grid=(bsz, n_q // ATT_TQ),
        in_specs=[pl.BlockSpec((1, ATT_TQ, q_rank), lambda b, i: (b, i, 0)),
                  pl.BlockSpec((1, n, N_KV_IN), lambda b, i: (b, 0, 0)),
                  full((n, HEAD_PAD)), full((n, HEAD_PAD)),
                  full((1, q_rank)), per_layer(wqa), per_layer(wqb),
                  full((1, MLA_KV_RANK)), per_layer(wk), per_layer(wv),
                  full((MLA_ROPE, hw)), full((MLA_ROPE, hw))],
        out_specs=pl.BlockSpec((1, ATT_TQ, GROUP_W), lambda b, i: (b, i, 0)),
        out_shape=jax.ShapeDtypeStruct((bsz, n_q, GROUP_W), bf16),
        scratch_shapes=[pltpu.VMEM((n, hw), bf16), pltpu.VMEM((MLA_HEADS * MLA_V, n), bf16)],
        compiler_params=_cparams(("parallel", "arbitrary")),
        name="mla",
    )(p_q, p_kv, cos, sin, qn_g.reshape(1, -1), wqa, wqb, kvn_g.reshape(1, -1), wk, wv, e1, e2)


MLP_TF = 1024
TAIL_GROUP_ROWS = 256


def _tail_kernel(yc_ref, yp_ref, ya_ref, y0_ref, y1_ref, bonus_ref, gate_ref, x_ref, m_ref, lg_ref, lb_ref,
                 gavg_ref, g_ref, wo_ref, w1_ref, w2_ref, fg_ref, o_ref, x1_scr, h2_scr, acc_scr,
                 *, n_lat, n_rows, tm, final_norm):
    kf = pl.program_id(2)
    is_ctx = _is_ctx_rows(pl.program_id(1), tm, n_lat, n_rows)

    @pl.when(kf == 0)
    def _():
        groups = [slice(g0, g0 + TAIL_GROUP_ROWS) for g0 in range(0, tm, TAIL_GROUP_ROWS)]
        mod = lambda j, g: _mod_row(m_ref, j, None if is_ctx is None else is_ctx[g])
        ysum = [y0_ref[0, g, :] + y1_ref[0, g, :] for g in groups]
        dl = [y - _dot_split(y, gavg_ref[...], 2) for y in ysum]
        var = [_dot_split(d * d, gavg_ref[...], 2) for d in dl]
        o = [d * lax.rsqrt(v + RW_LNX_EPS) * lg_ref[...] + lb_ref[...] for d, v in zip(dl, var)]
        y_rw = [((t + bonus_ref[0, g, :]) * gate_ref[0, g, :]).astype(bf16) for t, g in zip(o, groups)]
        y = [jnp.concatenate([yc_ref[0, g, :], t, yp_ref[0, g, :], ya_ref[0, g, :]], axis=1)
             for t, g in zip(y_rw, groups)]
        proj = [jnp.dot(t, wo_ref[...], preferred_element_type=f32) for t in y]
        x1 = [x_ref[0, g, :] + mod(2, g) * t for t, g in zip(proj, groups)]
        h2 = [_rms(t, g_ref[...]) * (1.0 + mod(4, g)) + mod(3, g) for t, g in zip(x1, groups)]
        for t, u, g in zip(x1, h2, groups):
            x1_scr[g, :] = t
            h2_scr[g, :] = u.astype(bf16)
        acc_scr[...] = jnp.zeros_like(acc_scr)

    z = jnp.maximum(jnp.dot(h2_scr[...], w1_ref[...], preferred_element_type=f32), 0.0)
    acc_scr[...] += jnp.dot((z * z).astype(bf16), w2_ref[...], preferred_element_type=f32)

    @pl.when(kf == pl.num_programs(2) - 1)
    def _():
        x2 = x1_scr[...] + _mod_row(m_ref, 5, is_ctx) * acc_scr[...]
        o_ref[0] = _rms(x2, fg_ref[...]) if final_norm else x2


def _tail_call(y_conv, y_pool, y_att, y0, y1, bonus, gate, xx, modall, lnx_g, lnx_b, g, w_out, w1, w2, final_g,
               layer, n_lat, n_rows, final_norm):
    bsz, _, d = xx.shape
    dff = w1.shape[2]
    tm = next(t for t in (1024, 768, 512, 256) if n_rows % t == 0)
    row = lambda w: pl.BlockSpec((1, tm, w), lambda b, i, k: (b, i, 0))
    vec = lambda w: pl.BlockSpec((1, w), lambda b, i, k: (0, 0))
    return pl.pallas_call(
        functools.partial(_tail_kernel, n_lat=n_lat, n_rows=n_rows, tm=tm, final_norm=final_norm),
        grid=(bsz, n_rows // tm, dff // MLP_TF),
        in_specs=[row(GROUP_W)] * 7 + [row(d),
                  pl.BlockSpec((1, 12, d), lambda b, i, k: (b, 0, 0)),
                  vec(GROUP_W), vec(GROUP_W),
                  pl.BlockSpec((GROUP_W, GROUP_W), lambda b, i, k: (0, 0)),
                  vec(d),
                  pl.BlockSpec((None, d, d), lambda b, i, k: (layer, 0, 0)),
                  pl.BlockSpec((None, d, MLP_TF), lambda b, i, k: (layer, 0, k)),
                  pl.BlockSpec((None, MLP_TF, d), lambda b, i, k: (layer, k, 0)),
                  vec(d)],
        out_specs=row(d),
        out_shape=jax.ShapeDtypeStruct((bsz, n_rows, d), f32),
        scratch_shapes=[pltpu.VMEM((tm, d), f32), pltpu.VMEM((tm, d), bf16), pltpu.VMEM((tm, d), f32)],
        compiler_params=_cparams(("parallel", "parallel", "arbitrary")),
        name="tail",
    )(y_conv, y_pool, y_att, y0, y1, bonus, gate, xx, modall, lnx_g.reshape(1, -1), lnx_b.reshape(1, -1),
      _group_avg_matrix(GROUP_W, RW_HEAD).astype(bf16), g.reshape(1, d), w_out, w1, w2, final_g.reshape(1, d))


def kernel(x, c, ctx, c_ctx, ada_w, ada_b, norm1_g, norm2_g, w_in, w_out, conv_dw, conv_db, conv_gn_g, conv_gn_b, conv_pw, pool_w, pool_scale, rw_mu_prev, rw_mu_next, rw_w0, rw_w2, rw_a0, rw_a2, rw_g2, rw_kk, rw_ka, rw_rk, rw_lnx_g, rw_lnx_b, mla_qn_g, mla_wuq, mla_kvn_g, mla_wukv, mlp_w1, mlp_w2, final_g):
    bsz, n_lat, d = x.shape
    n_ctx = ctx.shape[1]
    depth = ada_w.shape[0]
    n_all = n_ctx + n_lat
    assert n_ctx % max(RW_TILE, ATT_TQ, CONV_ROWS) == 0 and n_lat % max(RW_TILE, ATT_TQ, CONV_ROWS) == 0
    assert n_lat % GRID_W == 0 and bsz + 1 <= 24

    xx = jnp.concatenate([x, ctx], axis=1)
    s_all = jnp.zeros((24, d), f32).at[:bsz].set(c).at[bsz].set(c_ctx)
    cos, sin = _rope_tables(n_ctx, n_lat)
    w_in_pad = jnp.pad(w_in, ((0, 0), (0, 0), (0, (-P_IN) % LANES))).astype(bf16)
    w_out_b, w1_b, w2_b = w_out.astype(bf16), mlp_w1.astype(bf16), mlp_w2.astype(bf16)
    mla_w = _mla_weights(mla_wuq, mla_wukv)

    for l in range(depth):
        last = l == depth - 1
        mod = _ada_call(s_all, ada_w, ada_b, l)
        mod_lat = mod[:bsz].reshape(bsz, 6, d)
        mod_ctx = jnp.broadcast_to(mod[bsz].reshape(1, 6, d), (bsz, 6, d))
        modall = jnp.concatenate([mod_ctx, mod_lat], axis=1)

        n_rows = n_lat if last else n_all
        segs = ((0, n_lat),) if last else ((0, n_lat), (n_lat, n_ctx))
        p_conv, p_pool, p_q, p_rw, p_kv = _inproj_call(xx, modall, norm1_g[l], w_in_pad, l, n_lat)
        y_conv = _conv_call(p_conv, conv_dw[l], conv_db[l], conv_gn_g[l], conv_gn_b[l], conv_pw[l], segs)
        y_pool = _pool_call(p_pool, pool_w[l], pool_scale[l], segs)
        prep = _rwprep_call(p_rw, rw_mu_prev[l], rw_mu_next[l], rw_w0[l], rw_w2[l], rw_a0[l], rw_a2[l],
                            rw_g2[l], rw_kk[l], rw_ka[l], rw_rk[l], n_lat)
        y0, y1 = _rwscan_call(prep, n_ctx)
        y_att = _mla_call(p_q, p_kv, cos, sin, mla_qn_g[l], mla_kvn_g[l], mla_w, l, n_lat, n_rows)
        xx = _tail_call(y_conv, y_pool, y_att, y0, y1, prep[17], prep[18], xx, modall, rw_lnx_g[l], rw_lnx_b[l],
                        norm2_g[l], w_out_b, w1_b, w2_b, final_g, l, n_lat, n_rows, last)
    return xx
```

```python
import functools

import numpy as np
import jax
import jax.numpy as jnp
from jax import lax
from jax.experimental import pallas as pl
from jax.experimental.pallas import tpu as pltpu

f32 = jnp.float32
bf16 = jnp.bfloat16
HIGHEST = lax.Precision.HIGHEST

GROUP_W = 256
NORM_EPS = 1e-6
GN_EPS = 1e-5
CONV_K = 31
CONV_NORM_GROUPS = 4
RW_HEAD = 64
RW_HEADS = GROUP_W // RW_HEAD
RW_DECAY_LORA = 64
RW_A_LORA = 64
RW_GATE_LORA = 128
RW_LNX_EPS = 64e-5
POOL_WINDOWS = (2, 4, 8, 16)
MLA_HEADS = 4
MLA_NOPE = 64
MLA_ROPE = 32
MLA_V = 64
MLA_KV_RANK = 128
ROPE_BASE = 10000.0
GRID_W = 64
N_CONV_IN = 2 * GROUP_W
N_RW_IN = 3 * GROUP_W + RW_GATE_LORA + 2 * RW_DECAY_LORA + 2 * RW_A_LORA
N_KV_IN = MLA_KV_RANK + MLA_ROPE
OFF_POOL = N_CONV_IN
OFF_Q = OFF_POOL + GROUP_W
OFF_RW = OFF_Q + GROUP_W
OFF_KV = OFF_RW + N_RW_IN
P_IN = OFF_KV + N_KV_IN

LANES = 128
SUBLANES = 8
VMEM_LIMIT_BYTES = 56 * 1024 * 1024

RW_CHUNK = 64
HEAD_PAD = 128


def _cparams(sem):
    return pltpu.CompilerParams(dimension_semantics=sem, vmem_limit_bytes=VMEM_LIMIT_BYTES)


def _dot(a, b):
    return jnp.dot(a.astype(bf16), b.astype(bf16), preferred_element_type=f32)


def _dot_nt(a, b):
    return lax.dot_general(a.astype(bf16), b.astype(bf16), (((1,), (1,)), ((), ())), preferred_element_type=f32)


def _dot_tn(a, b):
    return lax.dot_general(a.astype(bf16), b.astype(bf16), (((0,), (0,)), ((), ())), preferred_element_type=f32)


def _dot_split(a, b, terms):
    split_lhs = a.dtype == f32
    x = a if split_lhs else b
    acc = None
    for _ in range(terms):
        piece = x.astype(bf16)
        part = jnp.dot(piece if split_lhs else a, b if split_lhs else piece, preferred_element_type=f32)
        acc = part if acc is None else acc + part
        x = x - piece.astype(f32)
    return acc


def _dot_3pass(a, b):
    a_hi, b_hi = a.astype(bf16), b.astype(bf16)
    a_lo, b_lo = (a - a_hi.astype(f32)).astype(bf16), (b - b_hi.astype(f32)).astype(bf16)
    return (jnp.dot(a_hi, b_hi, preferred_element_type=f32) + jnp.dot(a_lo, b_hi, preferred_element_type=f32)
            + jnp.dot(a_hi, b_lo, preferred_element_type=f32))


def _sigmoid(x):
    return jax.nn.sigmoid(x)


def _mod_row(m_ref, j, is_ctx):
    lat = m_ref[0, 6 + j:7 + j, :]
    if is_ctx is None:
        return lat
    return jnp.where(is_ctx, m_ref[0, j:j + 1, :], lat)


def _is_ctx_rows(tile_idx, tm, n_lat, n_rows):
    if n_rows <= n_lat:
        return None
    row = tile_idx * tm + lax.broadcasted_iota(jnp.int32, (tm, 1), 0)
    return row >= n_lat


def _group_stats_norm(y, gavg, eps):
    dlt = y - _dot_split(y, gavg, 2)
    return dlt * lax.rsqrt(_dot_split(dlt * dlt, gavg, 2) + eps)


def _rms(x, g):
    return x * lax.rsqrt(jnp.mean(x * x, axis=-1, keepdims=True) + NORM_EPS) * g


def _ada_kernel(s_ref, w_ref, b_ref, o_ref):
    s = s_ref[...]
    s = s * _sigmoid(s)
    o_ref[...] = _dot_3pass(s, w_ref[...]) + b_ref[...]


def _ada_call(s_all, w, b, layer):
    rows, d = s_all.shape
    depth, _, n = w.shape
    tn = 1536
    return pl.pallas_call(
        _ada_kernel,
        grid=(n // tn,),
        in_specs=[pl.BlockSpec((rows, d), lambda j: (0, 0)),
                  pl.BlockSpec((None, d, tn), lambda j: (layer, 0, j)),
                  pl.BlockSpec((None, 1, tn), lambda j: (layer, 0, j))],
        out_specs=pl.BlockSpec((rows, tn), lambda j: (0, j)),
        out_shape=jax.ShapeDtypeStruct((rows, n), f32),
        compiler_params=_cparams(("arbitrary",)),
        name="ada",
    )(s_all, w, b.reshape(depth, 1, n))


def _inproj_kernel(x_ref, m_ref, g_ref, w_ref, oc_ref, op_ref, oq_ref, orw_ref, okv_ref, *, n_lat, n_rows, tm):
    is_ctx = _is_ctx_rows(pl.program_id(1), tm, n_lat, n_rows)
    h = _rms(x_ref[0], g_ref[...]) * (1.0 + _mod_row(m_ref, 1, is_ctx)) + _mod_row(m_ref, 0, is_ctx)
    p = jnp.dot(h.astype(bf16), w_ref[...], preferred_element_type=f32)
    oc_ref[0] = p[:, 0:OFF_POOL]
    op_ref[0] = p[:, OFF_POOL:OFF_Q]
    oq_ref[0] = p[:, OFF_Q:OFF_RW]
    orw_ref[0] = p[:, OFF_RW:OFF_KV]
    okv_ref[0] = p[:, OFF_KV:P_IN]


def _inproj_call(xx, modall, g, w_pad, layer, n_lat):
    bsz, n, d = xx.shape
    tm = next(t for t in (768, 512, 384, 256) if n % t == 0)
    widths = (N_CONV_IN, GROUP_W, GROUP_W, N_RW_IN, N_KV_IN)
    return pl.pallas_call(
        functools.partial(_inproj_kernel, n_lat=n_lat, n_rows=n, tm=tm),
        grid=(bsz, n // tm),
        in_specs=[pl.BlockSpec((1, tm, d), lambda b, i: (b, i, 0)),
                  pl.BlockSpec((1, 12, d), lambda b, i: (b, 0, 0)),
                  pl.BlockSpec((1, d), lambda b, i: (0, 0)),
                  pl.BlockSpec((None,) + w_pad.shape[1:], lambda b, i: (layer, 0, 0))],
        out_specs=[pl.BlockSpec((1, tm, w), lambda b, i: (b, i, 0)) for w in widths],
        out_shape=[jax.ShapeDtypeStruct((bsz, n, w), f32) for w in widths],
        compiler_params=_cparams(("parallel", "parallel")),
        name="inproj",
    )(xx, modall, g.reshape(1, d), w_pad)


CONV_ROWS = 128
CONV_HALO = 16
CONV_NORM_ROWS = 256
CONV_NORM_GROUPS_PER_STEP = 4
POOL_CHUNKS_PER_STEP = 4


def _conv_kernel(p_ref, dw_ref, db_ref, gg_ref, gb_ref, pw_ref, gavg_ref, o_ref, u_scr, y_scr, *, segs):
    r, hl = CONV_ROWS, CONV_HALO
    win_rows = r + 2 * hl
    for s0, n in segs:
        u_scr[0:hl, :] = jnp.zeros((hl, GROUP_W), f32)
        u_scr[hl + n:hl + n + hl, :] = jnp.zeros((hl, GROUP_W), f32)

        def fill(c, carry, s0=s0):
            r0 = pl.multiple_of(c * r, r)
            blk = p_ref[0, pl.ds(s0 + r0, r), :]
            u_scr[pl.ds(hl + r0, r), :] = blk[:, :GROUP_W] * _sigmoid(blk[:, GROUP_W:])
            return carry

        lax.fori_loop(0, n // r, fill, 0)

        def taps(c, carry):
            r0 = pl.multiple_of(c * r, r)
            win = u_scr[pl.ds(r0, win_rows), :]
            rolled = [win] + [pltpu.roll(win, win_rows - b, axis=0) for b in range(1, SUBLANES)]
            acc = jnp.zeros((r, GROUP_W), f32)
            for j in range(CONV_K):
                off = hl - CONV_K // 2 + j
                base = off - off % SUBLANES
                acc = acc + rolled[off % SUBLANES][base:base + r] * dw_ref[j:j + 1, :]
            y_scr[pl.ds(r0, r), :] = acc + db_ref[...]
            return carry

        lax.fori_loop(0, n // r, taps, 0)

        gr = CONV_NORM_ROWS
        per_step = CONV_NORM_GROUPS_PER_STEP if n % (gr * CONV_NORM_GROUPS_PER_STEP) == 0 else 1

        def norm_project(c, carry, s0=s0, per_step=per_step):
            starts = [pl.multiple_of((c * per_step + g) * gr, gr) for g in range(per_step)]
            ys = [y_scr[pl.ds(t0, gr), :] for t0 in starts]
            dl = [y - _dot_split(y, gavg_ref[...], 2) for y in ys]
            var = [_dot_split(d * d, gavg_ref[...], 2) for d in dl]
            yn = [d * lax.rsqrt(v + GN_EPS) * gg_ref[...] + gb_ref[...] for d, v in zip(dl, var)]
            out = [_dot(t * _sigmoid(t), pw_ref[...]).astype(bf16) for t in yn]
            for t0, o in zip(starts, out):
                o_ref[0, pl.ds(s0 + t0, gr), :] = o
            return carry

        lax.fori_loop(0, n // (gr * per_step), norm_project, 0)


def _group_avg_matrix(width, group):
    idx = np.arange(width) // group
    return jnp.asarray((idx[:, None] == idx[None, :]).astype(np.float32) / group)


def _conv_call(p_conv, dw, db, gg, gb, pw, segs):
    bsz, n, _ = p_conv.shape
    n_out = max(s[0] + s[1] for s in segs)
    gavg = _group_avg_matrix(GROUP_W, GROUP_W // CONV_NORM_GROUPS).astype(bf16)
    max_seg = max(s[1] for s in segs)
    vec = lambda: pl.BlockSpec((1, GROUP_W), lambda b: (0, 0))
    return pl.pallas_call(
        functools.partial(_conv_kernel, segs=segs),
        grid=(bsz,),
        in_specs=[pl.BlockSpec((1, n, N_CONV_IN), lambda b: (b, 0, 0)),
                  pl.BlockSpec((CONV_K, GROUP_W), lambda b: (0, 0)),
                  vec(), vec(), vec(),
                  pl.BlockSpec((GROUP_W, GROUP_W), lambda b: (0, 0)),
                  pl.BlockSpec((GROUP_W, GROUP_W), lambda b: (0, 0))],
        out_specs=pl.BlockSpec((1, n_out, GROUP_W), lambda b: (b, 0, 0)),
        out_shape=jax.ShapeDtypeStruct((bsz, n_out, GROUP_W), bf16),
        scratch_shapes=[pltpu.VMEM((max_seg + 2 * CONV_HALO, GROUP_W), f32),
                        pltpu.VMEM((max_seg, GROUP_W), f32)],
        compiler_params=_cparams(("parallel",)),
        name="conv",
    )(p_conv, dw, db.reshape(1, -1), gg.reshape(1, -1), gb.reshape(1, -1), pw.astype(bf16), gavg)


def _pool_kernel(p_ref, w_ref, sc_ref, o_ref, u_scr, *, segs):
    r, hl = CONV_ROWS, CONV_HALO
    win_rows = r + 2 * hl
    pool_ch = GROUP_W // len(POOL_WINDOWS)
    lane = lax.broadcasted_iota(jnp.int32, (1, GROUP_W), 1)
    half = jnp.full((1, GROUP_W), POOL_WINDOWS[-1] // 2, jnp.int32)
    for gi in range(len(POOL_WINDOWS) - 2, -1, -1):
        half = jnp.where(lane < (gi + 1) * pool_ch, POOL_WINDOWS[gi] // 2, half)

    def shifted(v, k):
        return pltpu.roll(v, (win_rows - k) % win_rows, axis=0)

    for s0, n in segs:
        u_scr[0:hl, :] = jnp.zeros((hl, GROUP_W), f32)
        u_scr[hl + n:hl + n + hl, :] = jnp.zeros((hl, GROUP_W), f32)

        def fill(c, carry, s0=s0):
            r0 = pl.multiple_of(c * r, r)
            u_scr[pl.ds(hl + r0, r), :] = p_ref[0, pl.ds(s0 + r0, r), :]
            return carry

        lax.fori_loop(0, n // r, fill, 0)

        def deviation(r0, n=n):
            win = u_scr[pl.ds(r0, win_rows), :]
            s2 = win + shifted(win, -1)
            s4 = shifted(s2, -1) + shifted(s2, 1)
            s8 = shifted(s4, -2) + shifted(s4, 2)
            s16 = shifted(s8, -4) + shifted(s8, 4)
            sums = (s2, s4, s8, s16)
            sel = sums[-1]
            for gi in range(len(POOL_WINDOWS) - 2, -1, -1):
                sel = jnp.where(lane < (gi + 1) * pool_ch, sums[gi], sel)
            sel = sel[hl:hl + r]
            u = win[hl:hl + r]
            t = r0 + lax.broadcasted_iota(jnp.int32, (r, 1), 0)
            cnt = (jnp.minimum(t + half, n) - jnp.maximum(t - half, 0)).astype(f32)
            return sel / cnt - u

        per_step = max(t for t in range(1, POOL_CHUNKS_PER_STEP + 1) if (n // r) % t == 0)

        def body(c, carry, s0=s0, per_step=per_step):
            starts = [pl.multiple_of((c * per_step + g) * r, r) for g in range(per_step)]
            dlt = [deviation(r0) for r0 in starts]
            out = [(_dot(d, w_ref[...]) * sc_ref[...]).astype(bf16) for d in dlt]
            for r0, o in zip(starts, out):
                o_ref[0, pl.ds(s0 + r0, r), :] = o
            return carry

        lax.fori_loop(0, n // (r * per_step), body, 0)


def _block_diag(blocks):
    g, a, b = blocks.shape
    out = jnp.zeros((g * a, g * b), blocks.dtype)
    for i in range(g):
        out = out.at[i * a:(i + 1) * a, i * b:(i + 1) * b].set(blocks[i])
    return out


def _pool_call(p_pool, pool_w, pool_scale, segs):
    bsz, n, _ = p_pool.shape
    n_out = max(s[0] + s[1] for s in segs)
    max_seg = max(s[1] for s in segs)
    return pl.pallas_call(
        functools.partial(_pool_kernel, segs=segs),
        grid=(bsz,),
        in_specs=[pl.BlockSpec((1, n, GROUP_W), lambda b: (b, 0, 0)),
                  pl.BlockSpec((GROUP_W, GROUP_W), lambda b: (0, 0)),
                  pl.BlockSpec((1, GROUP_W), lambda b: (0, 0))],
        out_specs=pl.BlockSpec((1, n_out, GROUP_W), lambda b: (b, 0, 0)),
        out_shape=jax.ShapeDtypeStruct((bsz, n_out, GROUP_W), bf16),
        scratch_shapes=[pltpu.VMEM((max_seg + 2 * CONV_HALO, GROUP_W), f32)],
        compiler_params=_cparams(("parallel",)),
        name="pool",
    )(p_pool, _block_diag(pool_w).astype(bf16), pool_scale.reshape(1, -1))


RW_TILE = 256
RW_R, RW_K, RW_V, RW_G = 0, GROUP_W, 2 * GROUP_W, 3 * GROUP_W
RW_W = RW_G + RW_GATE_LORA
RW_A = RW_W + 2 * RW_DECAY_LORA
RW_STACK = RW_HEADS * RW_CHUNK


def _stack_heads(x, hm):
    return jnp.concatenate([x] * RW_HEADS, axis=0) * hm


def _unstack_heads(z):
    c = z.shape[0] // RW_HEADS
    out = z[0:c]
    for h in range(1, RW_HEADS):
        out = out + z[h * c:(h + 1) * c]
    return out


def _scan_order_masks():
    ri = lax.broadcasted_iota(jnp.int32, (RW_STACK, RW_STACK), 0)
    ci = lax.broadcasted_iota(jnp.int32, (RW_STACK, RW_STACK), 1)
    return ((ri > ci, ri >= ci), (ri < ci, ri <= ci)), (ri == ci).astype(f32)


def _head_mask():
    m = np.arange(RW_STACK)[:, None] // RW_CHUNK == np.arange(GROUP_W)[None, :] // RW_HEAD
    return jnp.asarray(m.astype(np.float32))


def _rwprep_kernel(p_ref, hp_ref, hn_ref, mup_ref, mun_ref, w0_ref, w2_ref, a0_ref, a2_ref, g2_ref,
                   kk_ref, ka_ref, rk_ref, tril_ref, triu_ref, hsum_ref, hm_ref,
                   rt0, bt0, bp0, kp0, kh0, wc0, yp0, pc0, rt1, bt1, bp1, kp1, kh1, wc1, yp1, pc1,
                   v_out, bonus_out, gate_out, *, seg_starts, seg_ends):
    tr, c = RW_TILE, RW_CHUNK
    i = pl.program_id(1)
    row0 = i * tr
    first = functools.reduce(jnp.logical_or, [row0 == s for s in seg_starts])
    last = functools.reduce(jnp.logical_or, [row0 + tr == e for e in seg_ends])
    ridx = lax.broadcasted_iota(jnp.int32, (tr, 1), 0)

    def zcols(a, b):
        p = p_ref[0, :, a:b]
        prev_row = jnp.where(first, 0.0, hp_ref[0, SUBLANES - 1:SUBLANES, a:b])
        next_row = jnp.where(last, 0.0, hn_ref[0, 0:1, a:b])
        prev = jnp.where(ridx == 0, prev_row, pltpu.roll(p, 1, axis=0))
        nxt = jnp.where(ridx == tr - 1, next_row, pltpu.roll(p, tr - 1, axis=0))
        return p + mup_ref[:, a:b] * (prev - p) + mun_ref[:, a:b] * (nxt - p)

    r = zcols(RW_R, RW_K)
    k = zcols(RW_K, RW_V)
    v = zcols(RW_V, RW_G)
    vb = v.astype(bf16)
    v_out[0] = vb
    gate_out[0] = _dot(_sigmoid(zcols(RW_G, RW_W)), g2_ref[...])
    kk = k * kk_ref[...]
    kk = kk / jnp.maximum(jnp.sqrt(_dot_split(kk * kk, hsum_ref[...], 2)), 1e-12)
    w_all = _dot_3pass(jnp.tanh(zcols(RW_W, RW_A)), w2_ref[...]) + w0_ref[...]
    a_all = _sigmoid(_dot_3pass(zcols(RW_A, N_RW_IN), a2_ref[...]) + a0_ref[...])
    kd_sum = jnp.zeros_like(k)
    outs = ((rt0, bt0, bp0, kp0, kh0, wc0, yp0, pc0, tril_ref), (rt1, bt1, bp1, kp1, kh1, wc1, yp1, pc1, triu_ref))
    scaled = []
    for d, (rt_o, bt_o, bp_o, kp_o, _, _, _, pc_o, tri_ref) in enumerate(outs):
        x = w_all[:, d * GROUP_W:(d + 1) * GROUP_W]
        neg = -x
        log_w = -(jnp.maximum(neg, 0.0) + jnp.log1p(jnp.exp(-jnp.abs(neg)))) - 0.5
        lw = -jnp.exp(log_w)
        a = a_all[:, d * GROUP_W:(d + 1) * GROUP_W]
        kd = k * (1.0 + (a - 1.0) * ka_ref[...])
        kd_sum = kd_sum + kd
        b = kk * a
        cum = _dot_split(tri_ref[...], lw, 3)
        tot_rows = []
        for ci in range(tr // c):
            edge = ci * c + (c - 1 if d == 0 else 0)
            tot_rows.append(cum[edge:edge + 1, :])
            pc_o[0, ci * SUBLANES:(ci + 1) * SUBLANES, :] = jnp.broadcast_to(jnp.exp(tot_rows[-1]), (SUBLANES, GROUP_W))
        tot = jnp.concatenate([jnp.broadcast_to(t, (c, GROUP_W)) for t in tot_rows], axis=0)
        e_neg = jnp.exp(-cum)
        e_rest = jnp.exp(tot - cum)
        rt = (r * jnp.exp(cum)).astype(bf16)
        kq = (kk * jnp.exp(cum - lw)).astype(bf16)
        bt = (b * e_neg).astype(bf16)
        kt = (kd * e_neg).astype(bf16)
        rt_o[0] = rt
        bt_o[0] = bt
        bp_o[0] = (b * e_rest).astype(bf16)
        kp_o[0] = (kd * e_rest).astype(bf16)
        scaled.append((rt, kq, bt, kt))
    bonus_out[0] = _dot_split(r * rk_ref[...] * kd_sum, hsum_ref[...], 2) * v

    hm = hm_ref[...]
    masks, eye = _scan_order_masks()
    chains = [(d, ci) for d in range(2) for ci in range(tr // c)]
    rows = lambda ci: slice(ci * c, (ci + 1) * c)
    rs = [_stack_heads(scaled[d][0][rows(ci)], hm) for d, ci in chains]
    ks = [_stack_heads(scaled[d][1][rows(ci)], hm) for d, ci in chains]
    bs = [_stack_heads(scaled[d][2][rows(ci)], hm) for d, ci in chains]
    kts = [_stack_heads(scaled[d][3][rows(ci)], hm) for d, ci in chains]
    vs = [_stack_heads(vb[rows(ci)], hm) for d, ci in chains]
    a_ub = [jnp.where(masks[d][0], _dot_nt(ks[q], bs[q]), 0.0) for q, (d, ci) in enumerate(chains)]
    a_vk = [jnp.where(masks[d][0], _dot_nt(ks[q], kts[q]), 0.0) for q, (d, ci) in enumerate(chains)]
    a_rk = [jnp.where(masks[d][1], _dot_nt(rs[q], kts[q]), 0.0) for q, (d, ci) in enumerate(chains)]
    tinv = [eye - a for a in a_ub]
    apow = a_ub
    for _ in range(int(np.log2(c)) - 1):
        apow = [_dot(a, a) for a in apow]
        tinv = [t + _dot(t, a) for t, a in zip(tinv, apow)]
    for q, (d, ci) in enumerate(chains):
        kh_o, wc_o, yp_o = outs[d][4], outs[d][5], outs[d][6]
        kh_o[0, rows(ci), :] = _unstack_heads(_dot(tinv[q], ks[q])).astype(bf16)
        wc_o[0, rows(ci), :] = _unstack_heads(_dot(tinv[q], _dot(a_vk[q], vs[q])))
        yp_o[0, rows(ci), :] = _unstack_heads(_dot(a_rk[q], vs[q]))


def _chunk_tri(tile, chunk, upper):
    t = np.arange(tile)
    same = (t[:, None] // chunk) == (t[None, :] // chunk)
    tri = (t[None, :] >= t[:, None]) if upper else (t[None, :] <= t[:, None])
    return jnp.asarray((same & tri).astype(np.float32))


def _rwprep_call(p_rw, mu_prev, mu_next, w0, w2, a0, a2, g2, kkp, kap, rk, n_lat):
    bsz, n, _ = p_rw.shape
    tr = RW_TILE
    nb8 = n // SUBLANES
    seg_starts = tuple(sorted({0, n_lat} - {n}))
    seg_ends = tuple(sorted({n_lat, n}))
    zeros = jnp.zeros((RW_DECAY_LORA, GROUP_W), f32)
    w2cat = jnp.concatenate([jnp.concatenate([w2[0], zeros], 1), jnp.concatenate([zeros, w2[1]], 1)], 0)
    a2cat = jnp.concatenate([jnp.concatenate([a2[0], zeros], 1), jnp.concatenate([zeros, a2[1]], 1)], 0)
    hsum = _group_avg_matrix(GROUP_W, RW_HEAD) * RW_HEAD
    full = lambda shape: pl.BlockSpec(shape, lambda b, i: tuple(0 for _ in shape))
    row_spec = pl.BlockSpec((1, tr, GROUP_W), lambda b, i: (b, i, 0))
    pc_rows = tr // RW_CHUNK * SUBLANES
    pc_spec = pl.BlockSpec((1, pc_rows, GROUP_W), lambda b, i: (b, i, 0))
    arr = lambda dt: jax.ShapeDtypeStruct((bsz, n, GROUP_W), dt)
    pc_arr = jax.ShapeDtypeStruct((bsz, n // RW_CHUNK * SUBLANES, GROUP_W), f32)
    dir_specs = [row_spec] * 7 + [pc_spec]
    dir_shapes = [arr(bf16)] * 5 + [arr(f32), arr(f32), pc_arr]
    return pl.pallas_call(
        functools.partial(_rwprep_kernel, seg_starts=seg_starts, seg_ends=seg_ends),
        grid=(bsz, n // tr),
        in_specs=[pl.BlockSpec((1, tr, N_RW_IN), lambda b, i: (b, i, 0)),
                  pl.BlockSpec((1, SUBLANES, N_RW_IN),
                               lambda b, i: (b, jnp.maximum(i * (tr // SUBLANES) - 1, 0), 0)),
                  pl.BlockSpec((1, SUBLANES, N_RW_IN),
                               lambda b, i: (b, jnp.minimum((i + 1) * (tr // SUBLANES), nb8 - 1), 0)),
                  full((1, N_RW_IN)), full((1, N_RW_IN)),
                  full((1, 2 * GROUP_W)), full((2 * RW_DECAY_LORA, 2 * GROUP_W)),
                  full((1, 2 * GROUP_W)), full((2 * RW_A_LORA, 2 * GROUP_W)),
                  full((RW_GATE_LORA, GROUP_W)),
                  full((1, GROUP_W)), full((1, GROUP_W)), full((1, GROUP_W)),
                  full((tr, tr)), full((tr, tr)), full((GROUP_W, GROUP_W)), full((RW_STACK, GROUP_W))],
        out_specs=dir_specs * 2 + [row_spec] * 3,
        out_shape=dir_shapes * 2 + [arr(bf16), arr(f32), arr(f32)],
        compiler_params=_cparams(("parallel", "parallel")),
        name="rwprep",
    )(p_rw, p_rw, p_rw, mu_prev.reshape(1, -1), mu_next.reshape(1, -1),
      w0.reshape(1, -1), w2cat, a0.reshape(1, -1), a2cat, g2.astype(bf16),
      kkp.reshape(1, -1), kap.reshape(1, -1), rk.reshape(1, -1),
      _chunk_tri(tr, RW_CHUNK, False).astype(bf16), _chunk_tri(tr, RW_CHUNK, True).astype(bf16),
      hsum.astype(bf16), _head_mask().astype(bf16))


RW_SCAN_BATCH = 8


def _rwscan_kernel(*refs, nb):
    nd = 9
    dir_refs = (refs[0:nd], refs[nd:2 * nd])
    hm_ref, y_refs, s_scr = refs[2 * nd], refs[2 * nd + 1:2 * nd + 3], refs[2 * nd + 3]

    @pl.when(pl.program_id(1) == 0)
    def _():
        s_scr[...] = jnp.zeros_like(s_scr)

    hm = hm_ref[...]
    hm32 = hm.astype(f32)
    masks, _ = _scan_order_masks()
    chains = [(d, bb) for bb in range(nb) for d in range(2)]
    ld = lambda d, bb, k: dir_refs[d][k][bb]
    rs = [_stack_heads(ld(d, bb, 0), hm) for d, bb in chains]
    bs = [_stack_heads(ld(d, bb, 1), hm) for d, bb in chains]
    bps = [_stack_heads(ld(d, bb, 2), hm) for d, bb in chains]
    kps = [_stack_heads(ld(d, bb, 3), hm) for d, bb in chains]
    khs = [_stack_heads(ld(d, bb, 4), hm) for d, bb in chains]
    w2t = [_stack_heads(ld(d, bb, 5), hm32).T for d, bb in chains]
    vs = [_stack_heads(ld(d, bb, 8), hm) for d, bb in chains]
    s = [s_scr[d, bb] for d, bb in chains]
    sb = [x.astype(bf16) for x in s]
    us_t = [-(_dot_nt(sb[j], khs[j]) + w2t[j]) for j in range(len(chains))]
    a_rb = [jnp.where(masks[d][1], _dot_nt(rs[j], bs[j]), 0.0) for j, (d, bb) in enumerate(chains)]
    ds = [_dot(us_t[j], bps[j]) + _dot_tn(vs[j], kps[j]) for j in range(len(chains))]
    for j, (d, bb) in enumerate(chains):
        pc_rows = jnp.concatenate([ld(d, bb, 7)] * (GROUP_W // SUBLANES), axis=0)
        s_scr[d, bb] = s[j] * pc_rows + ds[j]
    ys = [_dot_nt(rs[j], sb[j]) + _dot(a_rb[j], us_t[j].T) for j in range(len(chains))]
    for j, (d, bb) in enumerate(chains):
        y_refs[d][bb] = _unstack_heads(ys[j]) + ld(d, bb, 6)


def _rwscan_call(prep, n_ctx):
    v = prep[16]
    bsz, n, _ = v.shape
    c = RW_CHUNK
    nb = max(t for t in range(1, RW_SCAN_BATCH + 1) if bsz % t == 0)
    n_chunks = n // c
    ctx_chunks = n_ctx // c
    lat_chunks = n_chunks - ctx_chunks

    def fwd(b, i):
        return (b, jnp.where(i < ctx_chunks, lat_chunks + i, i - ctx_chunks), 0)

    def bwd(b, i):
        return (b, n_chunks - 1 - i, 0)

    blk = (nb, c, GROUP_W)
    pcb = (nb, SUBLANES, GROUP_W)
    dir_specs = lambda im: [pl.BlockSpec(blk, im)] * 7 + [pl.BlockSpec(pcb, im), pl.BlockSpec(blk, im)]
    return pl.pallas_call(
        functools.partial(_rwscan_kernel, nb=nb),
        grid=(bsz // nb, n_chunks),
        in_specs=dir_specs(fwd) + dir_specs(bwd) + [pl.BlockSpec((RW_STACK, GROUP_W), lambda b, i: (0, 0))],
        out_specs=[pl.BlockSpec(blk, fwd), pl.BlockSpec(blk, bwd)],
        out_shape=[jax.ShapeDtypeStruct((bsz, n, GROUP_W), f32)] * 2,
        scratch_shapes=[pltpu.VMEM((2, nb, GROUP_W, GROUP_W), f32)],
        compiler_params=_cparams(("parallel", "arbitrary")),
        name="rwscan",
    )(*prep[0:8], v, *prep[8:16], v, _head_mask().astype(bf16))


ATT_TQ = 256
ATT_KV_ROWS = 256


def _mla_kernel(pq_ref, pkv_ref, cos_ref, sin_ref, qg_ref, wqa_ref, wqb_ref, kvg_ref, wk_ref, wv_ref,
                e1_ref, e2_ref, o_ref, k_scr, vt_scr, *, n_lat, n_all, n_q):
    qi = pl.program_id(1)
    tq = ATT_TQ
    hp = HEAD_PAD

    def tile4(t):
        return jnp.concatenate([t] * MLA_HEADS, axis=1)

    @pl.when(qi == 0)
    def _():
        for r0 in range(0, n_all, ATT_KV_ROWS):
            pkv = pkv_ref[0, r0:r0 + ATT_KV_ROWS, :]
            ckv = _rms(pkv[:, :MLA_KV_RANK], kvg_ref[...])
            kr = pkv[:, MLA_KV_RANK:]
            cos = tile4(cos_ref[r0:r0 + ATT_KV_ROWS, :])
            sin = tile4(sin_ref[r0:r0 + ATT_KV_ROWS, :])
            kmat = _dot(ckv, wk_ref[...]) + _dot(kr, e1_ref[...]) * cos + _dot(kr, e2_ref[...]) * sin
            k_scr[r0:r0 + ATT_KV_ROWS, :] = kmat.astype(bf16)
            vt_scr[:, r0:r0 + ATT_KV_ROWS] = _dot(ckv, wv_ref[...]).T.astype(bf16)

    r0 = pl.multiple_of(qi * tq, tq)
    qn = _rms(pq_ref[0], qg_ref[...]).astype(bf16)
    cos = tile4(cos_ref[pl.ds(r0, tq), :])
    sin = tile4(sin_ref[pl.ds(r0, tq), :])
    scale = float(MLA_NOPE + MLA_ROPE) ** -0.5
    q = (jnp.dot(qn, wqa_ref[...], preferred_element_type=f32) * cos
         + jnp.dot(qn, wqb_ref[...], preferred_element_type=f32) * sin) * scale
    q = q.astype(bf16)

    def attend(k0, k1):
        heads = range(MLA_HEADS)
        st = [_dot_nt(k_scr[k0:k1, h * hp:(h + 1) * hp], q[:, h * hp:(h + 1) * hp]) for h in heads]
        e = [jnp.exp(st[h] - jnp.max(st[h], axis=0, keepdims=True)) for h in heads]
        inv = [1.0 / jnp.sum(e[h], axis=0, keepdims=True) for h in heads]
        outs = [_dot(vt_scr[h * MLA_V:(h + 1) * MLA_V, k0:k1], e[h]) * inv[h] for h in heads]
        o_ref[0] = jnp.concatenate(outs, axis=0).T.astype(bf16)

    if n_q > n_lat:
        @pl.when(qi < n_lat // tq)
        def _():
            attend(0, n_all)

        @pl.when(qi >= n_lat // tq)
        def _():
            attend(n_lat, n_all)
    else:
        attend(0, n_all)


def _rope_rotation():
    quarter = MLA_ROPE // 4
    rot = np.zeros((MLA_ROPE, MLA_ROPE), np.float32)
    for axis in range(2):
        for f in range(quarter):
            first, second = axis * 2 * quarter + f, axis * 2 * quarter + quarter + f
            rot[second, first] = -1.0
            rot[first, second] = 1.0
    return rot


def _rope_tables(n_ctx, n_lat):
    rows = n_lat // GRID_W
    row = jnp.repeat(jnp.arange(rows), GRID_W).astype(f32)
    col = jnp.tile(jnp.arange(GRID_W), rows).astype(f32)
    n_freq = MLA_ROPE // 4
    freq = ROPE_BASE ** (-jnp.arange(n_freq, dtype=f32) / n_freq)
    ang = jnp.concatenate([row[:, None] * freq, row[:, None] * freq, col[:, None] * freq, col[:, None] * freq], 1)
    cos = jnp.ones((n_lat + n_ctx, HEAD_PAD), f32).at[:n_lat, MLA_NOPE:MLA_NOPE + MLA_ROPE].set(jnp.cos(ang))
    sin = jnp.zeros((n_lat + n_ctx, HEAD_PAD), f32).at[:n_lat, MLA_NOPE:MLA_NOPE + MLA_ROPE].set(jnp.sin(ang))
    return cos, sin


def _mla_weights(wuq, wukv):
    rot = _rope_rotation()
    hq = MLA_NOPE + MLA_ROPE
    hkv = MLA_NOPE + MLA_V
    hw = MLA_HEADS * HEAD_PAD
    pqa = np.zeros((MLA_HEADS * hq, hw), np.float32)
    pqb = np.zeros_like(pqa)
    pk = np.zeros((MLA_HEADS * hkv, hw), np.float32)
    pv = np.zeros((MLA_HEADS * hkv, MLA_HEADS * MLA_V), np.float32)
    e1 = np.zeros((MLA_ROPE, hw), np.float32)
    for h in range(MLA_HEADS):
        c0 = h * HEAD_PAD
        pqa[h * hq + np.arange(hq), c0 + np.arange(hq)] = 1.0
        pqb[h * hq + MLA_NOPE:(h + 1) * hq, c0 + MLA_NOPE:c0 + hq] = rot
        pk[h * hkv + np.arange(MLA_NOPE), c0 + np.arange(MLA_NOPE)] = 1.0
        pv[h * hkv + MLA_NOPE + np.arange(MLA_V), h * MLA_V + np.arange(MLA_V)] = 1.0
        e1[np.arange(MLA_ROPE), c0 + MLA_NOPE + np.arange(MLA_ROPE)] = 1.0
    place = lambda w, p: jnp.einsum("lij,jk->lik", w, jnp.asarray(p), precision=HIGHEST).astype(bf16)
    return (place(wuq, pqa), place(wuq, pqb), place(wukv, pk), place(wukv, pv),
            jnp.asarray(e1).astype(bf16), jnp.asarray(rot @ e1).astype(bf16))


def _mla_call(p_q, p_kv, cos, sin, qn_g, kvn_g, weights, layer, n_lat, n_q):
    bsz, n, q_rank = p_q.shape
    wqa, wqb, wk, wv, e1, e2 = weights
    hw = MLA_HEADS * HEAD_PAD
    full = lambda shape: pl.BlockSpec(shape, lambda b, i: tuple(0 for _ in shape))
    per_layer = lambda w: pl.BlockSpec((None,) + w.shape[1:], lambda b, i: (layer, 0, 0))
    return pl.pallas_call(
        functools.partial(_mla_kernel, n_lat=n_lat, n_all=n, n_q=n_q),
        grid=(bsz, n_q // ATT_TQ),
        in_specs=[pl.BlockSpec((1, ATT_TQ, q_rank), lambda b, i: (b, i, 0)),
                  pl.BlockSpec((1, n, N_KV_IN), lambda b, i: (b, 0, 0)),
                  full((n, HEAD_PAD)), full((n, HEAD_PAD)),
                  full((1, q_rank)), per_layer(wqa), per_layer(wqb),
                  full((1, MLA_KV_RANK)), per_layer(wk), per_layer(wv),
                  full((MLA_ROPE, hw)), full((MLA_ROPE, hw))],
        out_specs=pl.BlockSpec((1, ATT_TQ, GROUP_W), lambda b, i: (b, i, 0)),
        out_shape=jax.ShapeDtypeStruct((bsz, n_q, GROUP_W), bf16),
        scratch_shapes=[pltpu.VMEM((n, hw), bf16), pltpu.VMEM((MLA_HEADS * MLA_V, n), bf16)],
        compiler_params=_cparams(("parallel", "arbitrary")),
        name="mla",
    )(p_q, p_kv, cos, sin, qn_g.reshape(1, -1), wqa, wqb, kvn_g.reshape(1, -1), wk, wv, e1, e2)


MLP_TF = 1024
TAIL_GROUP_ROWS = 256


def _tail_kernel(yc_ref, yp_ref, ya_ref, y0_ref, y1_ref, bonus_ref, gate_ref, x_ref, m_ref, lg_ref, lb_ref,
                 gavg_ref, g_ref, wo_ref, w1_ref, w2_ref, fg_ref, o_ref, x1_scr, h2_scr, acc_scr,
                 *, n_lat, n_rows, tm, final_norm):
    kf = pl.program_id(2)
    is_ctx = _is_ctx_rows(pl.program_id(1), tm, n_lat, n_rows)

    @pl.when(kf == 0)
    def _():
        groups = [slice(g0, g0 + TAIL_GROUP_ROWS) for g0 in range(0, tm, TAIL_GROUP_ROWS)]
        mod = lambda j, g: _mod_row(m_ref, j, None if is_ctx is None else is_ctx[g])
        ysum = [y0_ref[0, g, :] + y1_ref[0, g, :] for g in groups]
        dl = [y - _dot_split(y, gavg_ref[...], 2) for y in ysum]
        var = [_dot_split(d * d, gavg_ref[...], 2) for d in dl]
        o = [d * lax.rsqrt(v + RW_LNX_EPS) * lg_ref[...] + lb_ref[...] for d, v in zip(dl, var)]
        y_rw = [((t + bonus_ref[0, g, :]) * gate_ref[0, g, :]).astype(bf16) for t, g in zip(o, groups)]
        y = [jnp.concatenate([yc_ref[0, g, :], t, yp_ref[0, g, :], ya_ref[0, g, :]], axis=1)
             for t, g in zip(y_rw, groups)]
        proj = [jnp.dot(t, wo_ref[...], preferred_element_type=f32) for t in y]
        x1 = [x_ref[0, g, :] + mod(2, g) * t for t, g in zip(proj, groups)]
        h2 = [_rms(t, g_ref[...]) * (1.0 + mod(4, g)) + mod(3, g) for t, g in zip(x1, groups)]
        for t, u, g in zip(x1, h2, groups):
            x1_scr[g, :] = t
            h2_scr[g, :] = u.astype(bf16)
        acc_scr[...] = jnp.zeros_like(acc_scr)

    z = jnp.maximum(jnp.dot(h2_scr[...], w1_ref[...], preferred_element_type=f32), 0.0)
    acc_scr[...] += jnp.dot((z * z).astype(bf16), w2_ref[...], preferred_element_type=f32)

    @pl.when(kf == pl.num_programs(2) - 1)
    def _():
        x2 = x1_scr[...] + _mod_row(m_ref, 5, is_ctx) * acc_scr[...]
        o_ref[0] = _rms(x2, fg_ref[...]) if final_norm else x2


def _tail_call(y_conv, y_pool, y_att, y0, y1, bonus, gate, xx, modall, lnx_g, lnx_b, g, w_out, w1, w2, final_g,
               layer, n_lat, n_rows, final_norm):
    bsz, _, d = xx.shape
    dff = w1.shape[2]
    tm = next(t for t in (1024, 768, 512, 256) if n_rows % t == 0)
    row = lambda w: pl.BlockSpec((1, tm, w), lambda b, i, k: (b, i, 0))
    vec = lambda w: pl.BlockSpec((1, w), lambda b, i, k: (0, 0))
    return pl.pallas_call(
        functools.partial(_tail_kernel, n_lat=n_lat, n_rows=n_rows, tm=tm, final_norm=final_norm),
        grid=(bsz, n_rows // tm, dff // MLP_TF),
        in_specs=[row(GROUP_W)] * 7 + [row(d),
                  pl.BlockSpec((1, 12, d), lambda b, i, k: (b, 0, 0)),
                  vec(GROUP_W), vec(GROUP_W),
                  pl.BlockSpec((GROUP_W, GROUP_W), lambda b, i, k: (0, 0)),
                  vec(d),
                  pl.BlockSpec((None, d, d), lambda b, i, k: (layer, 0, 0)),
                  pl.BlockSpec((None, d, MLP_TF), lambda b, i, k: (layer, 0, k)),
                  pl.BlockSpec((None, MLP_TF, d), lambda b, i, k: (layer, k, 0)),
                  vec(d)],
        out_specs=row(d),
        out_shape=jax.ShapeDtypeStruct((bsz, n_rows, d), f32),
        scratch_shapes=[pltpu.VMEM((tm, d), f32), pltpu.VMEM((tm, d), bf16), pltpu.VMEM((tm, d), f32)],
        compiler_params=_cparams(("parallel", "parallel", "arbitrary")),
        name="tail",
    )(y_conv, y_pool, y_att, y0, y1, bonus, gate, xx, modall, lnx_g.reshape(1, -1), lnx_b.reshape(1, -1),
      _group_avg_matrix(GROUP_W, RW_HEAD).astype(bf16), g.reshape(1, d), w_out, w1, w2, final_g.reshape(1, d))


def kernel(x, c, ctx, c_ctx, ada_w, ada_b, norm1_g, norm2_g, w_in, w_out, conv_dw, conv_db, conv_gn_g, conv_gn_b, conv_pw, pool_w, pool_scale, rw_mu_prev, rw_mu_next, rw_w0, rw_w2, rw_a0, rw_a2, rw_g2, rw_kk, rw_ka, rw_rk, rw_lnx_g, rw_lnx_b, mla_qn_g, mla_wuq, mla_kvn_g, mla_wukv, mlp_w1, mlp_w2, final_g):
    bsz, n_lat, d = x.shape
    n_ctx = ctx.shape[1]
    depth = ada_w.shape[0]
    n_all = n_ctx + n_lat
    assert n_ctx % max(RW_TILE, ATT_TQ, CONV_ROWS) == 0 and n_lat % max(RW_TILE, ATT_TQ, CONV_ROWS) == 0
    assert n_lat % GRID_W == 0 and bsz + 1 <= 24

    xx = jnp.concatenate([x, ctx], axis=1)
    s_all = jnp.zeros((24, d), f32).at[:bsz].set(c).at[bsz].set(c_ctx)
    cos, sin = _rope_tables(n_ctx, n_lat)
    w_in_pad = jnp.pad(w_in, ((0, 0), (0, 0), (0, (-P_IN) % LANES))).astype(bf16)
    w_out_b, w1_b, w2_b = w_out.astype(bf16), mlp_w1.astype(bf16), mlp_w2.astype(bf16)
    mla_w = _mla_weights(mla_wuq, mla_wukv)

    for l in range(depth):
        last = l == depth - 1
        mod = _ada_call(s_all, ada_w, ada_b, l)
        mod_lat = mod[:bsz].reshape(bsz, 6, d)
        mod_ctx = jnp.broadcast_to(mod[bsz].reshape(1, 6, d), (bsz, 6, d))
        modall = jnp.concatenate([mod_ctx, mod_lat], axis=1)

        n_rows = n_lat if last else n_all
        segs = ((0, n_lat),) if last else ((0, n_lat), (n_lat, n_ctx))
        p_conv, p_pool, p_q, p_rw, p_kv = _inproj_call(xx, modall, norm1_g[l], w_in_pad, l, n_lat)
        y_conv = _conv_call(p_conv, conv_dw[l], conv_db[l], conv_gn_g[l], conv_gn_b[l], conv_pw[l], segs)
        y_pool = _pool_call(p_pool, pool_w[l], pool_scale[l], segs)
        prep = _rwprep_call(p_rw, rw_mu_prev[l], rw_mu_next[l], rw_w0[l], rw_w2[l], rw_a0[l], rw_a2[l],
                            rw_g2[l], rw_kk[l], rw_ka[l], rw_rk[l], n_lat)
        y0, y1 = _rwscan_call(prep, n_ctx)
        y_att = _mla_call(p_q, p_kv, cos, sin, mla_qn_g[l], mla_kvn_g[l], mla_w, l, n_lat, n_rows)
        xx = _tail_call(y_conv, y_pool, y_att, y0, y1, prep[17], prep[18], xx, modall, rw_lnx_g[l], rw_lnx_b[l],
                        norm2_g[l], w_out_b, w1_b, w2_b, final_g, l, n_lat, n_rows, last)
    return xx
```

```python
import functools

import numpy as np
import jax
import jax.numpy as jnp
from jax import lax
from jax.experimental import pallas as pl
from jax.experimental.pallas import tpu as pltpu

f32 = jnp.float32
bf16 = jnp.bfloat16
HIGHEST = lax.Precision.HIGHEST

GROUP_W = 256
NORM_EPS = 1e-6
GN_EPS = 1e-5
CONV_K = 31
CONV_NORM_GROUPS = 4
RW_HEAD = 64
RW_HEADS = GROUP_W // RW_HEAD
RW_DECAY_LORA = 64
RW_A_LORA = 64
RW_GATE_LORA = 128
RW_LNX_EPS = 64e-5
POOL_WINDOWS = (2, 4, 8, 16)
MLA_HEADS = 4
MLA_NOPE = 64
MLA_ROPE = 32
MLA_V = 64
MLA_KV_RANK = 128
ROPE_BASE = 10000.0
GRID_W = 64
N_CONV_IN = 2 * GROUP_W
N_RW_IN = 3 * GROUP_W + RW_GATE_LORA + 2 * RW_DECAY_LORA + 2 * RW_A_LORA
N_KV_IN = MLA_KV_RANK + MLA_ROPE
OFF_POOL = N_CONV_IN
OFF_Q = OFF_POOL + GROUP_W
OFF_RW = OFF_Q + GROUP_W
OFF_KV = OFF_RW + N_RW_IN
P_IN = OFF_KV + N_KV_IN

LANES = 128
SUBLANES = 8
VMEM_LIMIT_BYTES = 56 * 1024 * 1024

RW_CHUNK = 64
HEAD_PAD = 128


def _cparams(sem):
    return pltpu.CompilerParams(dimension_semantics=sem, vmem_limit_bytes=VMEM_LIMIT_BYTES)


def _dot(a, b):
    return jnp.dot(a.astype(bf16), b.astype(bf16), preferred_element_type=f32)


def _dot_nt(a, b):
    return lax.dot_general(a.astype(bf16), b.astype(bf16), (((1,), (1,)), ((), ())), preferred_element_type=f32)


def _dot_tn(a, b):
    return lax.dot_general(a.astype(bf16), b.astype(bf16), (((0,), (0,)), ((), ())), preferred_element_type=f32)


def _dot_split(a, b, terms):
    split_lhs = a.dtype == f32
    x = a if split_lhs else b
    acc = None
    for _ in range(terms):
        piece = x.astype(bf16)
        part = jnp.dot(piece if split_lhs else a, b if split_lhs else piece, preferred_element_type=f32)
        acc = part if acc is None else acc + part
        x = x - piece.astype(f32)
    return acc


def _dot_3pass(a, b):
    a_hi, b_hi = a.astype(bf16), b.astype(bf16)
    a_lo, b_lo = (a - a_hi.astype(f32)).astype(bf16), (b - b_hi.astype(f32)).astype(bf16)
    return (jnp.dot(a_hi, b_hi, preferred_element_type=f32) + jnp.dot(a_lo, b_hi, preferred_element_type=f32)
            + jnp.dot(a_hi, b_lo, preferred_element_type=f32))


def _sigmoid(x):
    return jax.nn.sigmoid(x)


def _mod_row(m_ref, j, is_ctx):
    lat = m_ref[0, 6 + j:7 + j, :]
    if is_ctx is None:
        return lat
    return jnp.where(is_ctx, m_ref[0, j:j + 1, :], lat)


def _is_ctx_rows(tile_idx, tm, n_lat, n_rows):
    if n_rows <= n_lat:
        return None
    row = tile_idx * tm + lax.broadcasted_iota(jnp.int32, (tm, 1), 0)
    return row >= n_lat


def _group_stats_norm(y, gavg, eps):
    dlt = y - _dot_split(y, gavg, 2)
    return dlt * lax.rsqrt(_dot_split(dlt * dlt, gavg, 2) + eps)


def _rms(x, g):
    return x * lax.rsqrt(jnp.mean(x * x, axis=-1, keepdims=True) + NORM_EPS) * g


ADA_COLS = 1536


def _ada_kernel(s_ref, w_ref, b_ref, o_ref):
    s = s_ref[...]
    s = s * _sigmoid(s)
    o_ref[...] = _dot_3pass(s, w_ref[...]) + b_ref[...]


def _ada_call(s_all, w, b, layer):
    rows, d = s_all.shape
    depth, _, n = w.shape
    tn = ADA_COLS
    return pl.pallas_call(
        _ada_kernel,
        grid=(n // tn,),
        in_specs=[pl.BlockSpec((rows, d), lambda j: (0, 0)),
                  pl.BlockSpec((None, d, tn), lambda j: (layer, 0, j)),
                  pl.BlockSpec((None, 1, tn), lambda j: (layer, 0, j))],
        out_specs=pl.BlockSpec((rows, tn), lambda j: (0, j)),
        out_shape=jax.ShapeDtypeStruct((rows, n), f32),
        compiler_params=_cparams(("arbitrary",)),
        name="ada",
    )(s_all, w, b.reshape(depth, 1, n))


def _inproj_kernel(x_ref, m_ref, g_ref, w_ref, oc_ref, op_ref, oq_ref, orw_ref, okv_ref, *, n_lat, n_rows, tm):
    is_ctx = _is_ctx_rows(pl.program_id(1), tm, n_lat, n_rows)
    h = _rms(x_ref[0], g_ref[...]) * (1.0 + _mod_row(m_ref, 1, is_ctx)) + _mod_row(m_ref, 0, is_ctx)
    p = jnp.dot(h.astype(bf16), w_ref[...], preferred_element_type=f32)
    oc_ref[0] = p[:, 0:OFF_POOL]
    op_ref[0] = p[:, OFF_POOL:OFF_Q]
    oq_ref[0] = p[:, OFF_Q:OFF_RW]
    orw_ref[0] = p[:, OFF_RW:OFF_KV]
    okv_ref[0] = p[:, OFF_KV:P_IN]


def _inproj_call(xx, modall, g, w_pad, layer, n_lat):
    bsz, n, d = xx.shape
    tm = next(t for t in (768, 512, 384, 256) if n % t == 0)
    widths = (N_CONV_IN, GROUP_W, GROUP_W, N_RW_IN, N_KV_IN)
    return pl.pallas_call(
        functools.partial(_inproj_kernel, n_lat=n_lat, n_rows=n, tm=tm),
        grid=(bsz, n // tm),
        in_specs=[pl.BlockSpec((1, tm, d), lambda b, i: (b, i, 0)),
                  pl.BlockSpec((1, 12, d), lambda b, i: (b, 0, 0)),
                  pl.BlockSpec((1, d), lambda b, i: (0, 0)),
                  pl.BlockSpec((None,) + w_pad.shape[1:], lambda b, i: (layer, 0, 0))],
        out_specs=[pl.BlockSpec((1, tm, w), lambda b, i: (b, i, 0)) for w in widths],
        out_shape=[jax.ShapeDtypeStruct((bsz, n, w), f32) for w in widths],
        compiler_params=_cparams(("parallel", "parallel")),
        name="inproj",
    )(xx, modall, g.reshape(1, d), w_pad)


CONV_ROWS = 128
CONV_HALO = 16
CONV_NORM_ROWS = 256
CONV_NORM_GROUPS_PER_STEP = 4
POOL_CHUNKS_PER_STEP = 4


def _conv_kernel(p_ref, dw_ref, db_ref, gg_ref, gb_ref, pw_ref, gavg_ref, o_ref, u_scr, y_scr, *, segs):
    r, hl = CONV_ROWS, CONV_HALO
    win_rows = r + 2 * hl
    for s0, n in segs:
        u_scr[0:hl, :] = jnp.zeros((hl, GROUP_W), f32)
        u_scr[hl + n:hl + n + hl, :] = jnp.zeros((hl, GROUP_W), f32)

        def fill(c, carry, s0=s0):
            r0 = pl.multiple_of(c * r, r)
            blk = p_ref[0, pl.ds(s0 + r0, r), :]
            u_scr[pl.ds(hl + r0, r), :] = blk[:, :GROUP_W] * _sigmoid(blk[:, GROUP_W:])
            return carry

        lax.fori_loop(0, n // r, fill, 0)

        def taps(c, carry):
            r0 = pl.multiple_of(c * r, r)
            win = u_scr[pl.ds(r0, win_rows), :]
            rolled = [win] + [pltpu.roll(win, win_rows - b, axis=0) for b in range(1, SUBLANES)]
            acc = jnp.zeros((r, GROUP_W), f32)
            for j in range(CONV_K):
                off = hl - CONV_K // 2 + j
                base = off - off % SUBLANES
                acc = acc + rolled[off % SUBLANES][base:base + r] * dw_ref[j:j + 1, :]
            y_scr[pl.ds(r0, r), :] = acc + db_ref[...]
            return carry

        lax.fori_loop(0, n // r, taps, 0)

        gr = CONV_NORM_ROWS
        per_step = CONV_NORM_GROUPS_PER_STEP if n % (gr * CONV_NORM_GROUPS_PER_STEP) == 0 else 1

        def norm_project(c, carry, s0=s0, per_step=per_step):
            starts = [pl.multiple_of((c * per_step + g) * gr, gr) for g in range(per_step)]
            ys = [y_scr[pl.ds(t0, gr), :] for t0 in starts]
            dl = [y - _dot_split(y, gavg_ref[...], 2) for y in ys]
            var = [_dot_split(d * d, gavg_ref[...], 2) for d in dl]
            yn = [d * lax.rsqrt(v + GN_EPS) * gg_ref[...] + gb_ref[...] for d, v in zip(dl, var)]
            out = [_dot(t * _sigmoid(t), pw_ref[...]).astype(bf16) for t in yn]
            for t0, o in zip(starts, out):
                o_ref[0, pl.ds(s0 + t0, gr), :] = o
            return carry

        lax.fori_loop(0, n // (gr * per_step), norm_project, 0)


def _group_avg_matrix(width, group):
    idx = np.arange(width) // group
    return jnp.asarray((idx[:, None] == idx[None, :]).astype(np.float32) / group)


def _conv_call(p_conv, dw, db, gg, gb, pw, segs):
    bsz, n, _ = p_conv.shape
    n_out = max(s[0] + s[1] for s in segs)
    gavg = _group_avg_matrix(GROUP_W, GROUP_W // CONV_NORM_GROUPS).astype(bf16)
    max_seg = max(s[1] for s in segs)
    vec = lambda: pl.BlockSpec((1, GROUP_W), lambda b: (0, 0))
    return pl.pallas_call(
        functools.partial(_conv_kernel, segs=segs),
        grid=(bsz,),
        in_specs=[pl.BlockSpec((1, n, N_CONV_IN), lambda b: (b, 0, 0)),
                  pl.BlockSpec((CONV_K, GROUP_W), lambda b: (0, 0)),
                  vec(), vec(), vec(),
                  pl.BlockSpec((GROUP_W, GROUP_W), lambda b: (0, 0)),
                  pl.BlockSpec((GROUP_W, GROUP_W), lambda b: (0, 0))],
        out_specs=pl.BlockSpec((1, n_out, GROUP_W), lambda b: (b, 0, 0)),
        out_shape=jax.ShapeDtypeStruct((bsz, n_out, GROUP_W), bf16),
        scratch_shapes=[pltpu.VMEM((max_seg + 2 * CONV_HALO, GROUP_W), f32),
                        pltpu.VMEM((max_seg, GROUP_W), f32)],
        compiler_params=_cparams(("parallel",)),
        name="conv",
    )(p_conv, dw, db.reshape(1, -1), gg.reshape(1, -1), gb.reshape(1, -1), pw.astype(bf16), gavg)


def _pool_kernel(p_ref, w_ref, sc_ref, o_ref, u_scr, *, segs):
    r, hl = CONV_ROWS, CONV_HALO
    win_rows = r + 2 * hl
    pool_ch = GROUP_W // len(POOL_WINDOWS)
    lane = lax.broadcasted_iota(jnp.int32, (1, GROUP_W), 1)
    half = jnp.full((1, GROUP_W), POOL_WINDOWS[-1] // 2, jnp.int32)
    for gi in range(len(POOL_WINDOWS) - 2, -1, -1):
        half = jnp.where(lane < (gi + 1) * pool_ch, POOL_WINDOWS[gi] // 2, half)

    def shifted(v, k):
        return pltpu.roll(v, (win_rows - k) % win_rows, axis=0)

    for s0, n in segs:
        u_scr[0:hl, :] = jnp.zeros((hl, GROUP_W), f32)
        u_scr[hl + n:hl + n + hl, :] = jnp.zeros((hl, GROUP_W), f32)

        def fill(c, carry, s0=s0):
            r0 = pl.multiple_of(c * r, r)
            u_scr[pl.ds(hl + r0, r), :] = p_ref[0, pl.ds(s0 + r0, r), :]
            return carry

        lax.fori_loop(0, n // r, fill, 0)

        def deviation(r0, n=n):
            win = u_scr[pl.ds(r0, win_rows), :]
            s2 = win + shifted(win, -1)
            s4 = shifted(s2, -1) + shifted(s2, 1)
            s8 = shifted(s4, -2) + shifted(s4, 2)
            s16 = shifted(s8, -4) + shifted(s8, 4)
            sums = (s2, s4, s8, s16)
            sel = sums[-1]
            for gi in range(len(POOL_WINDOWS) - 2, -1, -1):
                sel = jnp.where(lane < (gi + 1) * pool_ch, sums[gi], sel)
            sel = sel[hl:hl + r]
            u = win[hl:hl + r]
            t = r0 + lax.broadcasted_iota(jnp.int32, (r, 1), 0)
            cnt = (jnp.minimum(t + half, n) - jnp.maximum(t - half, 0)).astype(f32)
            return sel / cnt - u

        per_step = max(t for t in range(1, POOL_CHUNKS_PER_STEP + 1) if (n // r) % t == 0)

        def body(c, carry, s0=s0, per_step=per_step):
            starts = [pl.multiple_of((c * per_step + g) * r, r) for g in range(per_step)]
            dlt = [deviation(r0) for r0 in starts]
            out = [(_dot(d, w_ref[...]) * sc_ref[...]).astype(bf16) for d in dlt]
            for r0, o in zip(starts, out):
                o_ref[0, pl.ds(s0 + r0, r), :] = o
            return carry

        lax.fori_loop(0, n // (r * per_step), body, 0)


def _block_diag(blocks):
    g, a, b = blocks.shape
    eye = jnp.eye(g, dtype=blocks.dtype)
    return (blocks[:, :, None, :] * eye[:, None, :, None]).reshape(g * a, g * b)


def _pool_call(p_pool, pool_w, pool_scale, segs):
    bsz, n, _ = p_pool.shape
    n_out = max(s[0] + s[1] for s in segs)
    max_seg = max(s[1] for s in segs)
    return pl.pallas_call(
        functools.partial(_pool_kernel, segs=segs),
        grid=(bsz,),
        in_specs=[pl.BlockSpec((1, n, GROUP_W), lambda b: (b, 0, 0)),
                  pl.BlockSpec((GROUP_W, GROUP_W), lambda b: (0, 0)),
                  pl.BlockSpec((1, GROUP_W), lambda b: (0, 0))],
        out_specs=pl.BlockSpec((1, n_out, GROUP_W), lambda b: (b, 0, 0)),
        out_shape=jax.ShapeDtypeStruct((bsz, n_out, GROUP_W), bf16),
        scratch_shapes=[pltpu.VMEM((max_seg + 2 * CONV_HALO, GROUP_W), f32)],
        compiler_params=_cparams(("parallel",)),
        name="pool",
    )(p_pool, _block_diag(pool_w).astype(bf16), pool_scale.reshape(1, -1))


RW_TILE = 256
RW_R, RW_K, RW_V, RW_G = 0, GROUP_W, 2 * GROUP_W, 3 * GROUP_W
RW_W = RW_G + RW_GATE_LORA
RW_A = RW_W + 2 * RW_DECAY_LORA
RW_STACK = RW_HEADS * RW_CHUNK


def _stack_heads(x, hm):
    return jnp.concatenate([x] * RW_HEADS, axis=0) * hm


def _unstack_heads(z):
    c = z.shape[0] // RW_HEADS
    out = z[0:c]
    for h in range(1, RW_HEADS):
        out = out + z[h * c:(h + 1) * c]
    return out


def _scan_order_masks():
    ri = lax.broadcasted_iota(jnp.int32, (RW_STACK, RW_STACK), 0)
    ci = lax.broadcasted_iota(jnp.int32, (RW_STACK, RW_STACK), 1)
    return ((ri > ci, ri >= ci), (ri < ci, ri <= ci)), (ri == ci).astype(f32)


def _head_mask():
    m = np.arange(RW_STACK)[:, None] // RW_CHUNK == np.arange(GROUP_W)[None, :] // RW_HEAD
    return jnp.asarray(m.astype(np.float32))


def _rwprep_kernel(p_ref, hp_ref, hn_ref, mup_ref, mun_ref, w0_ref, w2_ref, a0_ref, a2_ref, g2_ref,
                   kk_ref, ka_ref, rk_ref, tril_ref, triu_ref, hsum_ref, hm_ref,
                   rt0, bt0, bp0, kp0, kh0, wc0, yp0, pc0, rt1, bt1, bp1, kp1, kh1, wc1, yp1, pc1,
                   v_out, bonus_out, gate_out, *, seg_starts, seg_ends):
    tr, c = RW_TILE, RW_CHUNK
    i = pl.program_id(1)
    row0 = i * tr
    first = functools.reduce(jnp.logical_or, [row0 == s for s in seg_starts])
    last = functools.reduce(jnp.logical_or, [row0 + tr == e for e in seg_ends])
    ridx = lax.broadcasted_iota(jnp.int32, (tr, 1), 0)

    def zcols(a, b):
        p = p_ref[0, :, a:b]
        prev_row = jnp.where(first, 0.0, hp_ref[0, SUBLANES - 1:SUBLANES, a:b])
        next_row = jnp.where(last, 0.0, hn_ref[0, 0:1, a:b])
        prev = jnp.where(ridx == 0, prev_row, pltpu.roll(p, 1, axis=0))
        nxt = jnp.where(ridx == tr - 1, next_row, pltpu.roll(p, tr - 1, axis=0))
        return p + mup_ref[:, a:b] * (prev - p) + mun_ref[:, a:b] * (nxt - p)

    r = zcols(RW_R, RW_K)
    k = zcols(RW_K, RW_V)
    v = zcols(RW_V, RW_G)
    vb = v.astype(bf16)
    v_out[0] = vb
    gate_out[0] = _dot(_sigmoid(zcols(RW_G, RW_W)), g2_ref[...])
    kk = k * kk_ref[...]
    kk = kk / jnp.maximum(jnp.sqrt(_dot_split(kk * kk, hsum_ref[...], 2)), 1e-12)
    w_all = _dot_3pass(jnp.tanh(zcols(RW_W, RW_A)), w2_ref[...]) + w0_ref[...]
    a_all = _sigmoid(_dot_3pass(zcols(RW_A, N_RW_IN), a2_ref[...]) + a0_ref[...])
    kd_sum = jnp.zeros_like(k)
    outs = ((rt0, bt0, bp0, kp0, kh0, wc0, yp0, pc0, tril_ref), (rt1, bt1, bp1, kp1, kh1, wc1, yp1, pc1, triu_ref))
    scaled = []
    for d, (rt_o, bt_o, bp_o, kp_o, _, _, _, pc_o, tri_ref) in enumerate(outs):
        x = w_all[:, d * GROUP_W:(d + 1) * GROUP_W]
        neg = -x
        log_w = -(jnp.maximum(neg, 0.0) + jnp.log1p(jnp.exp(-jnp.abs(neg)))) - 0.5
        lw = -jnp.exp(log_w)
        a = a_all[:, d * GROUP_W:(d + 1) * GROUP_W]
        kd = k * (1.0 + (a - 1.0) * ka_ref[...])
        kd_sum = kd_sum + kd
        b = kk * a
        cum = _dot_split(tri_ref[...], lw, 3)
        tot_rows = []
        for ci in range(tr // c):
            edge = ci * c + (c - 1 if d == 0 else 0)
            tot_rows.append(cum[edge:edge + 1, :])
            pc_o[0, ci * SUBLANES:(ci + 1) * SUBLANES, :] = jnp.broadcast_to(jnp.exp(tot_rows[-1]), (SUBLANES, GROUP_W))
        tot = jnp.concatenate([jnp.broadcast_to(t, (c, GROUP_W)) for t in tot_rows], axis=0)
        e_neg = jnp.exp(-cum)
        e_rest = jnp.exp(tot - cum)
        rt = (r * jnp.exp(cum)).astype(bf16)
        kq = (kk * jnp.exp(cum - lw)).astype(bf16)
        bt = (b * e_neg).astype(bf16)
        kt = (kd * e_neg).astype(bf16)
        rt_o[0] = rt
        bt_o[0] = bt
        bp_o[0] = (b * e_rest).astype(bf16)
        kp_o[0] = (kd * e_rest).astype(bf16)
        scaled.append((rt, kq, bt, kt))
    bonus_out[0] = _dot_split(r * rk_ref[...] * kd_sum, hsum_ref[...], 2) * v

    hm = hm_ref[...]
    masks, eye = _scan_order_masks()
    chains = [(d, ci) for d in range(2) for ci in range(tr // c)]
    rows = lambda ci: slice(ci * c, (ci + 1) * c)
    rs = [_stack_heads(scaled[d][0][rows(ci)], hm) for d, ci in chains]
    ks = [_stack_heads(scaled[d][1][rows(ci)], hm) for d, ci in chains]
    bs = [_stack_heads(scaled[d][2][rows(ci)], hm) for d, ci in chains]
    kts = [_stack_heads(scaled[d][3][rows(ci)], hm) for d, ci in chains]
    vs = [_stack_heads(vb[rows(ci)], hm) for d, ci in chains]
    a_ub = [jnp.where(masks[d][0], _dot_nt(ks[q], bs[q]), 0.0) for q, (d, ci) in enumerate(chains)]
    a_vk = [jnp.where(masks[d][0], _dot_nt(ks[q], kts[q]), 0.0) for q, (d, ci) in enumerate(chains)]
    a_rk = [jnp.where(masks[d][1], _dot_nt(rs[q], kts[q]), 0.0) for q, (d, ci) in enumerate(chains)]
    tinv = [eye - a for a in a_ub]
    apow = a_ub
    for _ in range(int(np.log2(c)) - 1):
        apow = [_dot(a, a) for a in apow]
        tinv = [t + _dot(t, a) for t, a in zip(tinv, apow)]
    for q, (d, ci) in enumerate(chains):
        kh_o, wc_o, yp_o = outs[d][4], outs[d][5], outs[d][6]
        kh_o[0, rows(ci), :] = _unstack_heads(_dot(tinv[q], ks[q])).astype(bf16)
        wc_o[0, rows(ci), :] = _unstack_heads(_dot(tinv[q], _dot(a_vk[q], vs[q])))
        yp_o[0, rows(ci), :] = _unstack_heads(_dot(a_rk[q], vs[q]))


def _chunk_tri(tile, chunk, upper):
    t = np.arange(tile)
    same = (t[:, None] // chunk) == (t[None, :] // chunk)
    tri = (t[None, :] >= t[:, None]) if upper else (t[None, :] <= t[:, None])
    return jnp.asarray((same & tri).astype(np.float32))


def _rwprep_call(p_rw, mu_prev, mu_next, w0, w2, a0, a2, g2, kkp, kap, rk, n_lat):
    bsz, n, _ = p_rw.shape
    tr = RW_TILE
    nb8 = n // SUBLANES
    seg_starts = tuple(sorted({0, n_lat} - {n}))
    seg_ends = tuple(sorted({n_lat, n}))
    w2cat, a2cat = _block_diag(w2), _block_diag(a2)
    hsum = _group_avg_matrix(GROUP_W, RW_HEAD) * RW_HEAD
    full = lambda shape: pl.BlockSpec(shape, lambda b, i: tuple(0 for _ in shape))
    row_spec = pl.BlockSpec((1, tr, GROUP_W), lambda b, i: (b, i, 0))
    pc_rows = tr // RW_CHUNK * SUBLANES
    pc_spec = pl.BlockSpec((1, pc_rows, GROUP_W), lambda b, i: (b, i, 0))
    arr = lambda dt: jax.ShapeDtypeStruct((bsz, n, GROUP_W), dt)
    pc_arr = jax.ShapeDtypeStruct((bsz, n // RW_CHUNK * SUBLANES, GROUP_W), f32)
    dir_specs = [row_spec] * 7 + [pc_spec]
    dir_shapes = [arr(bf16)] * 5 + [arr(f32), arr(f32), pc_arr]
    return pl.pallas_call(
        functools.partial(_rwprep_kernel, seg_starts=seg_starts, seg_ends=seg_ends),
        grid=(bsz, n // tr),
        in_specs=[pl.BlockSpec((1, tr, N_RW_IN), lambda b, i: (b, i, 0)),
                  pl.BlockSpec((1, SUBLANES, N_RW_IN),
                               lambda b, i: (b, jnp.maximum(i * (tr // SUBLANES) - 1, 0), 0)),
                  pl.BlockSpec((1, SUBLANES, N_RW_IN),
                               lambda b, i: (b, jnp.minimum((i + 1) * (tr // SUBLANES), nb8 - 1), 0)),
                  full((1, N_RW_IN)), full((1, N_RW_IN)),
                  full((1, 2 * GROUP_W)), full((2 * RW_DECAY_LORA, 2 * GROUP_W)),
                  full((1, 2 * GROUP_W)), full((2 * RW_A_LORA, 2 * GROUP_W)),
                  full((RW_GATE_LORA, GROUP_W)),
                  full((1, GROUP_W)), full((1, GROUP_W)), full((1, GROUP_W)),
                  full((tr, tr)), full((tr, tr)), full((GROUP_W, GROUP_W)), full((RW_STACK, GROUP_W))],
        out_specs=dir_specs * 2 + [row_spec] * 3,
        out_shape=dir_shapes * 2 + [arr(bf16), arr(f32), arr(f32)],
        compiler_params=_cparams(("parallel", "parallel")),
        name="rwprep",
    )(p_rw, p_rw, p_rw, mu_prev.reshape(1, -1), mu_next.reshape(1, -1),
      w0.reshape(1, -1), w2cat, a0.reshape(1, -1), a2cat, g2.astype(bf16),
      kkp.reshape(1, -1), kap.reshape(1, -1), rk.reshape(1, -1),
      _chunk_tri(tr, RW_CHUNK, False).astype(bf16), _chunk_tri(tr, RW_CHUNK, True).astype(bf16),
      hsum.astype(bf16), _head_mask().astype(bf16))


RW_SCAN_BATCH = 8


def _rwscan_kernel(*refs, nb):
    nd = 9
    dir_refs = (refs[0:nd], refs[nd:2 * nd])
    hm_ref, y_refs, s_scr = refs[2 * nd], refs[2 * nd + 1:2 * nd + 3], refs[2 * nd + 3]

    @pl.when(pl.program_id(1) == 0)
    def _():
        s_scr[...] = jnp.zeros_like(s_scr)

    hm = hm_ref[...]
    hm32 = hm.astype(f32)
    masks, _ = _scan_order_masks()
    chains = [(d, bb) for bb in range(nb) for d in range(2)]
    ld = lambda d, bb, k: dir_refs[d][k][bb]
    rs = [_stack_heads(ld(d, bb, 0), hm) for d, bb in chains]
    bs = [_stack_heads(ld(d, bb, 1), hm) for d, bb in chains]
    bps = [_stack_heads(ld(d, bb, 2), hm) for d, bb in chains]
    kps = [_stack_heads(ld(d, bb, 3), hm) for d, bb in chains]
    khs = [_stack_heads(ld(d, bb, 4), hm) for d, bb in chains]
    w2t = [_stack_heads(ld(d, bb, 5), hm32).T for d, bb in chains]
    vs = [_stack_heads(ld(d, bb, 8), hm) for d, bb in chains]
    s = [s_scr[d, bb] for d, bb in chains]
    sb = [x.astype(bf16) for x in s]
    us_t = [-(_dot_nt(sb[j], khs[j]) + w2t[j]) for j in range(len(chains))]
    a_rb = [jnp.where(masks[d][1], _dot_nt(rs[j], bs[j]), 0.0) for j, (d, bb) in enumerate(chains)]
    ds = [_dot(us_t[j], bps[j]) + _dot_tn(vs[j], kps[j]) for j in range(len(chains))]
    for j, (d, bb) in enumerate(chains):
        pc_rows = jnp.concatenate([ld(d, bb, 7)] * (GROUP_W // SUBLANES), axis=0)
        s_scr[d, bb] = s[j] * pc_rows + ds[j]
    ys = [_dot_nt(rs[j], sb[j]) + _dot(a_rb[j], us_t[j].T) for j in range(len(chains))]
    for j, (d, bb) in enumerate(chains):
        y_refs[d][bb] = _unstack_heads(ys[j]) + ld(d, bb, 6)


def _rwscan_call(prep, n_ctx):
    v = prep[16]
    bsz, n, _ = v.shape
    c = RW_CHUNK
    nb = max(t for t in range(1, RW_SCAN_BATCH + 1) if bsz % t == 0)
    n_chunks = n // c
    ctx_chunks = n_ctx // c
    lat_chunks = n_chunks - ctx_chunks

    def fwd(b, i):
        return (b, jnp.where(i < ctx_chunks, lat_chunks + i, i - ctx_chunks), 0)

    def bwd(b, i):
        return (b, n_chunks - 1 - i, 0)

    blk = (nb, c, GROUP_W)
    pcb = (nb, SUBLANES, GROUP_W)
    dir_specs = lambda im: [pl.BlockSpec(blk, im)] * 7 + [pl.BlockSpec(pcb, im), pl.BlockSpec(blk, im)]
    return pl.pallas_call(
        functools.partial(_rwscan_kernel, nb=nb),
        grid=(bsz // nb, n_chunks),
        in_specs=dir_specs(fwd) + dir_specs(bwd) + [pl.BlockSpec((RW_STACK, GROUP_W), lambda b, i: (0, 0))],
        out_specs=[pl.BlockSpec(blk, fwd), pl.BlockSpec(blk, bwd)],
        out_shape=[jax.ShapeDtypeStruct((bsz, n, GROUP_W), f32)] * 2,
        scratch_shapes=[pltpu.VMEM((2, nb, GROUP_W, GROUP_W), f32)],
        compiler_params=_cparams(("parallel", "arbitrary")),
        name="rwscan",
    )(*prep[0:8], v, *prep[8:16], v, _head_mask().astype(bf16))


ATT_TQ = 256
ATT_KV_ROWS = 256


def _mla_kernel(pq_ref, pkv_ref, cos_ref, sin_ref, qg_ref, wqa_ref, wqb_ref, kvg_ref, wk_ref, wv_ref,
                e1_ref, e2_ref, o_ref, k_scr, vt_scr, *, n_lat, n_all, n_q):
    qi = pl.program_id(1)
    tq = ATT_TQ
    hp = HEAD_PAD

    def tile4(t):
        return jnp.concatenate([t] * MLA_HEADS, axis=1)

    @pl.when(qi == 0)
    def _():
        for r0 in range(0, n_all, ATT_KV_ROWS):
            pkv = pkv_ref[0, r0:r0 + ATT_KV_ROWS, :]
            ckv = _rms(pkv[:, :MLA_KV_RANK], kvg_ref[...])
            kr = pkv[:, MLA_KV_RANK:]
            cos = tile4(cos_ref[r0:r0 + ATT_KV_ROWS, :])
            sin = tile4(sin_ref[r0:r0 + ATT_KV_ROWS, :])
            kmat = _dot(ckv, wk_ref[...]) + _dot(kr, e1_ref[...]) * cos + _dot(kr, e2_ref[...]) * sin
            k_scr[r0:r0 + ATT_KV_ROWS, :] = kmat.astype(bf16)
            vt_scr[:, r0:r0 + ATT_KV_ROWS] = _dot(ckv, wv_ref[...]).T.astype(bf16)

    r0 = pl.multiple_of(qi * tq, tq)
    qn = _rms(pq_ref[0], qg_ref[...]).astype(bf16)
    cos = tile4(cos_ref[pl.ds(r0, tq), :])
    sin = tile4(sin_ref[pl.ds(r0, tq), :])
    scale = float(MLA_NOPE + MLA_ROPE) ** -0.5
    q = (jnp.dot(qn, wqa_ref[...], preferred_element_type=f32) * cos
         + jnp.dot(qn, wqb_ref[...], preferred_element_type=f32) * sin) * scale
    q = q.astype(bf16)

    def attend(k0, k1):
        heads = range(MLA_HEADS)
        st = [_dot_nt(k_scr[k0:k1, h * hp:(h + 1) * hp], q[:, h * hp:(h + 1) * hp]) for h in heads]
        e = [jnp.exp(st[h] - jnp.max(st[h], axis=0, keepdims=True)) for h in heads]
        inv = [1.0 / jnp.sum(e[h], axis=0, keepdims=True) for h in heads]
        outs = [_dot(vt_scr[h * MLA_V:(h + 1) * MLA_V, k0:k1], e[h]) * inv[h] for h in heads]
        o_ref[0] = jnp.concatenate(outs, axis=0).T.astype(bf16)

    if n_q > n_lat:
        @pl.when(qi < n_lat // tq)
        def _():
            attend(0, n_all)

        @pl.when(qi >= n_lat // tq)
        def _():
            attend(n_lat, n_all)
    else:
        attend(0, n_all)


def _rope_rotation():
    quarter = MLA_ROPE // 4
    rot = np.zeros((MLA_ROPE, MLA_ROPE), np.float32)
    for axis in range(2):
        for f in range(quarter):
            first, second = axis * 2 * quarter + f, axis * 2 * quarter + quarter + f
            rot[second, first] = -1.0
            rot[first, second] = 1.0
    return rot


def _rope_tables(n_ctx, n_lat):
    rows = n_lat // GRID_W
    row = jnp.repeat(jnp.arange(rows), GRID_W).astype(f32)
    col = jnp.tile(jnp.arange(GRID_W), rows).astype(f32)
    n_freq = MLA_ROPE // 4
    freq = ROPE_BASE ** (-jnp.arange(n_freq, dtype=f32) / n_freq)
    ang = jnp.concatenate([row[:, None] * freq, row[:, None] * freq, col[:, None] * freq, col[:, None] * freq], 1)
    def table(rot, fill):
        left = jnp.full((n_lat, MLA_NOPE), fill, f32)
        right = jnp.full((n_lat, HEAD_PAD - MLA_NOPE - MLA_ROPE), fill, f32)
        lat = jnp.concatenate([left, rot, right], axis=1)
        return jnp.concatenate([lat, jnp.full((n_ctx, HEAD_PAD), fill, f32)], axis=0)

    return table(jnp.cos(ang), 1.0), table(jnp.sin(ang), 0.0)


def _mla_weights(wuq, wukv):
    rot = _rope_rotation()
    hq = MLA_NOPE + MLA_ROPE
    hkv = MLA_NOPE + MLA_V
    hw = MLA_HEADS * HEAD_PAD
    pqa = np.zeros((MLA_HEADS * hq, hw), np.float32)
    pqb = np.zeros_like(pqa)
    pk = np.zeros((MLA_HEADS * hkv, hw), np.float32)
    pv = np.zeros((MLA_HEADS * hkv, MLA_HEADS * MLA_V), np.float32)
    e1 = np.zeros((MLA_ROPE, hw), np.float32)
    for h in range(MLA_HEADS):
        c0 = h * HEAD_PAD
        pqa[h * hq + np.arange(hq), c0 + np.arange(hq)] = 1.0
        pqb[h * hq + MLA_NOPE:(h + 1) * hq, c0 + MLA_NOPE:c0 + hq] = rot
        pk[h * hkv + np.arange(MLA_NOPE), c0 + np.arange(MLA_NOPE)] = 1.0
        pv[h * hkv + MLA_NOPE + np.arange(MLA_V), h * MLA_V + np.arange(MLA_V)] = 1.0
        e1[np.arange(MLA_ROPE), c0 + MLA_NOPE + np.arange(MLA_ROPE)] = 1.0
    place = lambda w, p: jnp.einsum("lij,jk->lik", w, jnp.asarray(p), precision=HIGHEST).astype(bf16)
    return (place(wuq, pqa), place(wuq, pqb), place(wukv, pk), place(wukv, pv),
            jnp.asarray(e1).astype(bf16), jnp.asarray(rot @ e1).astype(bf16))


def _mla_call(p_q, p_kv, cos, sin, qn_g, kvn_g, weights, layer, n_lat, n_q):
    bsz, n, q_rank = p_q.shape
    wqa, wqb, wk, wv, e1, e2 = weights
    hw = MLA_HEADS * HEAD_PAD
    full = lambda shape: pl.BlockSpec(shape, lambda b, i: tuple(0 for _ in shape))
    per_layer = lambda w: pl.BlockSpec((None,) + w.shape[1:], lambda b, i: (layer, 0, 0))
    return pl.pallas_call(
        functools.partial(_mla_kernel, n_lat=n_lat, n_all=n, n_q=n_q),
        grid=(bsz, n_q // ATT_TQ),
        in_specs=[pl.BlockSpec((1, ATT_TQ, q_rank), lambda b, i: (b, i, 0)),
                  pl.BlockSpec((1, n, N_KV_IN), lambda b, i: (b, 0, 0)),
                  full((n, HEAD_PAD)), full((n, HEAD_PAD)),
                  full((1, q_rank)), per_layer(wqa), per_layer(wqb),
                  full((1, MLA_KV_RANK)), per_layer(wk), per_layer(wv),
                  full((MLA_ROPE, hw)), full((MLA_ROPE, hw))],
        out_specs=pl.BlockSpec((1, ATT_TQ, GROUP_W), lambda b, i: (b, i, 0)),
        out_shape=jax.ShapeDtypeStruct((bsz, n_q, GROUP_W), bf16),
        scratch_shapes=[pltpu.VMEM((n, hw), bf16), pltpu.VMEM((MLA_HEADS * MLA_V, n), bf16)],
        compiler_params=_cparams(("parallel", "arbitrary")),
        name="mla",
    )(p_q, p_kv, cos, sin, qn_g.reshape(1, -1), wqa, wqb, kvn_g.reshape(1, -1), wk, wv, e1, e2)


MLP_TF = 1024
TAIL_GROUP_ROWS = 256


def _tail_kernel(yc_ref, yp_ref, ya_ref, y0_ref, y1_ref, bonus_ref, gate_ref, x_ref, m_ref, lg_ref, lb_ref,
                 gavg_ref, g_ref, wo_ref, w1_ref, w2_ref, fg_ref, o_ref, x1_scr, h2_scr, acc_scr,
                 *, n_lat, n_rows, tm, final_norm):
    kf = pl.program_id(2)
    is_ctx = _is_ctx_rows(pl.program_id(1), tm, n_lat, n_rows)

    @pl.when(kf == 0)
    def _():
        groups = [slice(g0, g0 + TAIL_GROUP_ROWS) for g0 in range(0, tm, TAIL_GROUP_ROWS)]
        mod = lambda j, g: _mod_row(m_ref, j, None if is_ctx is None else is_ctx[g])
        ysum = [y0_ref[0, g, :] + y1_ref[0, g, :] for g in groups]
        dl = [y - _dot_split(y, gavg_ref[...], 2) for y in ysum]
        var = [_dot_split(d * d, gavg_ref[...], 2) for d in dl]
        o = [d * lax.rsqrt(v + RW_LNX_EPS) * lg_ref[...] + lb_ref[...] for d, v in zip(dl, var)]
        y_rw = [((t + bonus_ref[0, g, :]) * gate_ref[0, g, :]).astype(bf16) for t, g in zip(o, groups)]
        y = [jnp.concatenate([yc_ref[0, g, :], t, yp_ref[0, g, :], ya_ref[0, g, :]], axis=1)
             for t, g in zip(y_rw, groups)]
        proj = [jnp.dot(t, wo_ref[...], preferred_element_type=f32) for t in y]
        x1 = [x_ref[0, g, :] + mod(2, g) * t for t, g in zip(proj, groups)]
        h2 = [_rms(t, g_ref[...]) * (1.0 + mod(4, g)) + mod(3, g) for t, g in zip(x1, groups)]
        for t, u, g in zip(x1, h2, groups):
            x1_scr[g, :] = t
            h2_scr[g, :] = u.astype(bf16)
        acc_scr[...] = jnp.zeros_like(acc_scr)

    z = jnp.maximum(jnp.dot(h2_scr[...], w1_ref[...], preferred_element_type=f32), 0.0)
    acc_scr[...] += jnp.dot((z * z).astype(bf16), w2_ref[...], preferred_element_type=f32)

    @pl.when(kf == pl.num_programs(2) - 1)
    def _():
        x2 = x1_scr[...] + _mod_row(m_ref, 5, is_ctx) * acc_scr[...]
        o_ref[0] = _rms(x2, fg_ref[...]) if final_norm else x2


def _tail_call(y_conv, y_pool, y_att, y0, y1, bonus, gate, xx, modall, lnx_g, lnx_b, g, w_out, w1, w2, final_g,
               layer, n_lat, n_rows, final_norm):
    bsz, _, d = xx.shape
    dff = w1.shape[2]
    tm = next(t for t in (1024, 768, 512, 256) if n_rows % t == 0)
    row = lambda w: pl.BlockSpec((1, tm, w), lambda b, i, k: (b, i, 0))
    vec = lambda w: pl.BlockSpec((1, w), lambda b, i, k: (0, 0))
    return pl.pallas_call(
        functools.partial(_tail_kernel, n_lat=n_lat, n_rows=n_rows, tm=tm, final_norm=final_norm),
        grid=(bsz, n_rows // tm, dff // MLP_TF),
        in_specs=[row(GROUP_W)] * 7 + [row(d),
                  pl.BlockSpec((1, 12, d), lambda b, i, k: (b, 0, 0)),
                  vec(GROUP_W), vec(GROUP_W),
                  pl.BlockSpec((GROUP_W, GROUP_W), lambda b, i, k: (0, 0)),
                  vec(d),
                  pl.BlockSpec((None, d, d), lambda b, i, k: (layer, 0, 0)),
                  pl.BlockSpec((None, d, MLP_TF), lambda b, i, k: (layer, 0, k)),
                  pl.BlockSpec((None, MLP_TF, d), lambda b, i, k: (layer, k, 0)),
                  vec(d)],
        out_specs=row(d),
        out_shape=jax.ShapeDtypeStruct((bsz, n_rows, d), f32),
        scratch_shapes=[pltpu.VMEM((tm, d), f32), pltpu.VMEM((tm, d), bf16), pltpu.VMEM((tm, d), f32)],
        compiler_params=_cparams(("parallel", "parallel", "arbitrary")),
        name="tail",
    )(y_conv, y_pool, y_att, y0, y1, bonus, gate, xx, modall, lnx_g.reshape(1, -1), lnx_b.reshape(1, -1),
      _group_avg_matrix(GROUP_W, RW_HEAD).astype(bf16), g.reshape(1, d), w_out, w1, w2, final_g.reshape(1, d))


def kernel(x, c, ctx, c_ctx, ada_w, ada_b, norm1_g, norm2_g, w_in, w_out, conv_dw, conv_db, conv_gn_g, conv_gn_b, conv_pw, pool_w, pool_scale, rw_mu_prev, rw_mu_next, rw_w0, rw_w2, rw_a0, rw_a2, rw_g2, rw_kk, rw_ka, rw_rk, rw_lnx_g, rw_lnx_b, mla_qn_g, mla_wuq, mla_kvn_g, mla_wukv, mlp_w1, mlp_w2, final_g):
    bsz, n_lat, d = x.shape
    n_ctx = ctx.shape[1]
    depth = ada_w.shape[0]
    n_all = n_ctx + n_lat
    assert n_ctx % max(RW_TILE, ATT_TQ, CONV_ROWS) == 0 and n_lat % max(RW_TILE, ATT_TQ, CONV_ROWS) == 0
    assert n_lat % GRID_W == 0 and ada_w.shape[2] % ADA_COLS == 0

    xx = jnp.concatenate([x, ctx], axis=1)
    cond_rows = -(-(bsz + 1) // SUBLANES) * SUBLANES
    s_all = jnp.concatenate([c, c_ctx[None, :], jnp.zeros((cond_rows - bsz - 1, d), f32)], axis=0)
    cos, sin = _rope_tables(n_ctx, n_lat)
    w_in_pad = jnp.pad(w_in, ((0, 0), (0, 0), (0, (-P_IN) % LANES))).astype(bf16)
    w_out_b, w1_b, w2_b = w_out.astype(bf16), mlp_w1.astype(bf16), mlp_w2.astype(bf16)
    mla_w = _mla_weights(mla_wuq, mla_wukv)

    for l in range(depth):
        last = l == depth - 1
        mod = _ada_call(s_all, ada_w, ada_b, l)
        mod_lat = mod[:bsz].reshape(bsz, 6, d)
        mod_ctx = jnp.broadcast_to(mod[bsz].reshape(1, 6, d), (bsz, 6, d))
        modall = jnp.concatenate([mod_ctx, mod_lat], axis=1)

        n_rows = n_lat if last else n_all
        segs = ((0, n_lat),) if last else ((0, n_lat), (n_lat, n_ctx))
        p_conv, p_pool, p_q, p_rw, p_kv = _inproj_call(xx, modall, norm1_g[l], w_in_pad, l, n_lat)
        y_conv = _conv_call(p_conv, conv_dw[l], conv_db[l], conv_gn_g[l], conv_gn_b[l], conv_pw[l], segs)
        y_pool = _pool_call(p_pool, pool_w[l], pool_scale[l], segs)
        prep = _rwprep_call(p_rw, rw_mu_prev[l], rw_mu_next[l], rw_w0[l], rw_w2[l], rw_a0[l], rw_a2[l],
                            rw_g2[l], rw_kk[l], rw_ka[l], rw_rk[l], n_lat)
        y0, y1 = _rwscan_call(prep, n_ctx)
        y_att = _mla_call(p_q, p_kv, cos, sin, mla_qn_g[l], mla_kvn_g[l], mla_w, l, n_lat, n_rows)
        xx = _tail_call(y_conv, y_pool, y_att, y0, y1, prep[17], prep[18], xx, modall, rw_lnx_g[l], rw_lnx_b[l],
                        norm2_g[l], w_out_b, w1_b, w2_b, final_g, l, n_lat, n_rows, last)
    return xx
```
